```python
import math
import jax, jax.numpy as jnp
from jax import lax
import numpy as np

D_MODEL = 1024
BATCH = 8
SEQ = 8192
DEPTH = 1

ATTN_HEADS = 8
ATTN_HEAD_DIM = 64
ATTN_WIDTH = ATTN_HEADS * 2 * ATTN_HEAD_DIM
Q_BLOCK = 128
SSM_EXPAND = 2
D_INNER = SSM_EXPAND * D_MODEL
SSM_HEAD_DIM = 64
SSM_HEADS = D_INNER // SSM_HEAD_DIM
SSM_GROUPS = 4
SSM_STATE = 128
CONV_WIDTH = 4
CONV_DIM = D_INNER + 2 * SSM_GROUPS * SSM_STATE
SSM_CHUNK = 128
N_BRANCHES = 2
IN_PROJ_DIM = 3 * ATTN_WIDTH + D_INNER + CONV_DIM + SSM_HEADS + N_BRANCHES * D_MODEL
N_EXPERTS = 32
TOP_K = 4
D_FF = D_MODEL
SWIGLU_LIMIT = 7.0
SWIGLU_ALPHA = 1.702
MOE_ROW_BLOCK = 512
EPS = 1e-5
QK_EPS = 1e-6

kernel_name = "hybrid_diffattn_ssd_gated_moe"


def _rms_norm(x, gain, eps):
    xf = x.astype(jnp.float32)
    y = xf * lax.rsqrt(jnp.mean(xf * xf, axis=-1, keepdims=True) + eps)
    return (y * gain.astype(jnp.float32)).astype(x.dtype)


def _split_points():
    sizes = [ATTN_WIDTH, ATTN_WIDTH, ATTN_WIDTH, D_INNER, CONV_DIM, SSM_HEADS]
    pts, acc = [], 0
    for s in sizes:
        acc += s
        pts.append(acc)
    return pts


def _alibi_slopes(n):
    return jnp.power(2.0, -8.0 * (jnp.arange(n, dtype=jnp.float32) + 1.0) / n)


def _diff_attention(q, k, v, q_norm_w, k_norm_w, lq1, lk1, lq2, lk2, subln_w, lambda_init):
    Bsz, S, _ = q.shape
    H, dh = ATTN_HEADS, ATTN_HEAD_DIM
    q = _rms_norm(q.reshape(Bsz, S, H, 2, dh), q_norm_w, QK_EPS) * (dh ** -0.5)
    k = _rms_norm(k.reshape(Bsz, S, H, 2, dh), k_norm_w, QK_EPS)
    v = v.reshape(Bsz, S, H, 2 * dh)
    f32 = jnp.float32
    lam = (jnp.exp(jnp.sum(lq1.astype(f32) * lk1.astype(f32)))
           - jnp.exp(jnp.sum(lq2.astype(f32) * lk2.astype(f32))) + lambda_init)
    nb = S // Q_BLOCK
    qb = q.reshape(Bsz, nb, Q_BLOCK, H, 2, dh).transpose(1, 0, 3, 4, 2, 5)
    kt = k.transpose(0, 2, 3, 1, 4)
    vt = v.transpose(0, 2, 1, 3)
    slopes = _alibi_slopes(H)
    key_pos = jnp.arange(S)

    def block(args):
        qi, blk = args
        q_pos = blk * Q_BLOCK + jnp.arange(Q_BLOCK)
        dist = q_pos[:, None] - key_pos[None, :]
        s = jnp.einsum('bhcqd,bhckd->bhcqk', qi, kt).astype(f32)
        s = s - slopes[:, None, None, None] * dist.astype(f32)
        s = jnp.where(dist >= 0, s, -jnp.inf)
        p = jax.nn.softmax(s, axis=-1)
        a = p[:, :, 0] - lam * p[:, :, 1]
        return jnp.einsum('bhqk,bhkd->bhqd', a.astype(vt.dtype), vt)

    o = lax.map(block, (qb, jnp.arange(nb)))
    o = o.transpose(1, 0, 3, 2, 4).reshape(Bsz, S, H, 2 * dh)
    o = _rms_norm(o, subln_w, EPS) * (1.0 - lambda_init)
    return o.reshape(Bsz, S, ATTN_WIDTH)


def _ssd_scan(x, dt, A, Bm, Cm):
    Bsz, S, H, P = x.shape
    G, Hg, Q = SSM_GROUPS, SSM_HEADS // SSM_GROUPS, SSM_CHUNK
    nc = S // Q

    def to_chunks(t):
        return t.reshape(Bsz, nc, Q, *t.shape[2:]).swapaxes(0, 1)

    xc = to_chunks(x.reshape(Bsz, S, G, Hg, P))
    dtc = to_chunks(dt.reshape(Bsz, S, G, Hg))
    Bc, Cc = to_chunks(Bm), to_chunks(Cm)
    Ag = A.reshape(G, Hg)
    tri = jnp.tril(jnp.ones((Q, Q), dtype=bool))

    def step(state, inp):
        xq, dtq, Bq, Cq = inp
        cum = jnp.cumsum(dtq * Ag, axis=1)
        diff = cum[:, :, None] - cum[:, None, :]
        L = jnp.exp(jnp.where(tri[None, :, :, None, None], diff, -jnp.inf))
        xdt = xq * dtq[..., None]
        CB = jnp.einsum('btgn,bsgn->btsg', Cq, Bq)
        y_diag = jnp.einsum('btsg,btsgh,bsghp->btghp', CB, L, xdt)
        y_off = jnp.einsum('btgn,bghpn->btghp', Cq, state) * jnp.exp(cum)[..., None]
        decay = jnp.exp(cum[:, -1:] - cum)
        new_state = (state * jnp.exp(cum[:, -1])[..., None, None]
                     + jnp.einsum('bsgh,bsghp,bsgn->bghpn', decay, xdt, Bq))
        return new_state, y_diag + y_off

    state0 = jnp.zeros((Bsz, G, Hg, P, SSM_STATE), jnp.float32)
    _, ys = lax.scan(step, state0, (xc, dtc, Bc, Cc))
    return ys.swapaxes(0, 1).reshape(Bsz, S, H, P)


def _ssd_branch(z, xbc, dt_raw, conv_w, conv_b, dt_bias, a_log, d_skip, ssm_norm_w):
    Bsz, S, _ = xbc.shape
    f32 = jnp.float32
    xbc = lax.conv_general_dilated(
        xbc, conv_w[:, None, :].astype(xbc.dtype), window_strides=(1,),
        padding=[(CONV_WIDTH - 1, 0)], dimension_numbers=('NWC', 'WIO', 'NWC'),
        feature_group_count=CONV_DIM) + conv_b
    xbc = jax.nn.silu(xbc)
    xs, Bm, Cm = jnp.split(xbc, [D_INNER, D_INNER + SSM_GROUPS * SSM_STATE], axis=-1)
    dt = jax.nn.softplus(dt_raw.astype(f32) + dt_bias.astype(f32))
    A = -jnp.exp(a_log.astype(f32))
    xh = xs.reshape(Bsz, S, SSM_HEADS, SSM_HEAD_DIM).astype(f32)
    y = _ssd_scan(xh, dt, A,
                  Bm.reshape(Bsz, S, SSM_GROUPS, SSM_STATE).astype(f32),
                  Cm.reshape(Bsz, S, SSM_GROUPS, SSM_STATE).astype(f32))
    y = y + xh * d_skip.astype(f32)[:, None]
    y = y.reshape(Bsz, S, D_INNER).astype(z.dtype) * jax.nn.silu(z)
    gsz = D_INNER // SSM_GROUPS
    y = _rms_norm(y.reshape(Bsz, S, SSM_GROUPS, gsz), ssm_norm_w.reshape(SSM_GROUPS, gsz), EPS)
    return y.reshape(Bsz, S, D_INNER)


def _moe(h, w_router, b_router, w_gate_up, b_gate_up, w_down, b_down):
    Bsz, S, D = h.shape
    T = Bsz * S
    R = MOE_ROW_BLOCK
    hf = h.reshape(T, D)
    logits = (hf @ w_router + b_router).astype(jnp.float32)
    top_logits, top_idx = lax.top_k(logits, TOP_K)
    top_w = jax.nn.softmax(top_logits, axis=-1)
    M = T * TOP_K
    n_blk = M // R
    flat_e = top_idx.reshape(M)
    order = jnp.argsort(flat_e)
    sorted_e = flat_e[order]
    sorted_tok = order // TOP_K
    sorted_w = top_w.reshape(M)[order]
    counts = jnp.zeros((N_EXPERTS,), jnp.int32).at[flat_e].add(1)
    ends = jnp.cumsum(counts)
    starts = ends - counts
    first_blk = starts // R
    n_items = jnp.where(counts > 0, (ends - 1) // R - first_blk + 1, 0)
    item_end = jnp.cumsum(n_items)
    item_start = item_end - n_items
    n_work = n_blk + N_EXPERTS
    w_ids = jnp.arange(n_work)
    item_expert = jnp.minimum(jnp.searchsorted(item_end, w_ids, side='right'), N_EXPERTS - 1)
    item_block = jnp.clip(first_blk[item_expert] + w_ids - item_start[item_expert], 0, n_blk - 1)
    item_valid = w_ids < item_end[-1]
    item_rows = item_block[:, None] * R + jnp.arange(R)[None, :]
    item_tok = sorted_tok[item_rows]
    row_ok = (sorted_e[item_rows] == item_expert[:, None]) & item_valid[:, None]
    item_wt = jnp.where(row_ok, sorted_w[item_rows], 0.0)

    def one_item(args):
        e, tok, wt = args
        xr = hf[tok]
        gu = xr @ w_gate_up[e] + b_gate_up[e]
        glu, lin = jnp.split(gu, 2, axis=-1)
        glu = jnp.minimum(glu, SWIGLU_LIMIT)
        lin = jnp.clip(lin, -SWIGLU_LIMIT, SWIGLU_LIMIT)
        act = glu * jax.nn.sigmoid(SWIGLU_ALPHA * glu) * (lin + 1.0)
        y = act @ w_down[e] + b_down[e]
        return y * wt.astype(y.dtype)[:, None]

    ys = lax.map(one_item, (item_expert, item_tok, item_wt))
    out = jnp.zeros((T, D), h.dtype).at[item_tok.reshape(-1)].add(ys.reshape(-1, D).astype(h.dtype))
    return out.reshape(Bsz, S, D)


def setup_inputs(seed: int = 0) -> dict:
    key = jax.random.key(seed)
    ks = jax.random.split(key, 26)
    f32 = jnp.float32
    L = DEPTH

    def nrm(k, shape, scale):
        return jax.random.normal(k, shape, f32) * scale

    dt0 = jnp.exp(jax.random.uniform(ks[11], (L, SSM_HEADS), f32)
                  * (math.log(0.1) - math.log(0.001)) + math.log(0.001))
    return {
        "x": nrm(ks[0], (BATCH, SEQ, D_MODEL), 1.0),
        "norm1_w": 1.0 + nrm(ks[1], (L, D_MODEL), 0.02),
        "w_in": nrm(ks[2], (L, D_MODEL, IN_PROJ_DIM), D_MODEL ** -0.5),
        "q_norm_w": 1.0 + nrm(ks[3], (L, ATTN_HEAD_DIM), 0.02),
        "k_norm_w": 1.0 + nrm(ks[4], (L, ATTN_HEAD_DIM), 0.02),
        "lambda_q1": nrm(ks[5], (L, ATTN_HEAD_DIM), 0.1),
        "lambda_k1": nrm(ks[6], (L, ATTN_HEAD_DIM), 0.1),
        "lambda_q2": nrm(ks[7], (L, ATTN_HEAD_DIM), 0.1),
        "lambda_k2": nrm(ks[8], (L, ATTN_HEAD_DIM), 0.1),
        "subln_w": 1.0 + nrm(ks[9], (L, 2 * ATTN_HEAD_DIM), 0.02),
        "conv_w": nrm(ks[10], (L, CONV_WIDTH, CONV_DIM), CONV_WIDTH ** -0.5),
        "conv_b": nrm(ks[12], (L, CONV_DIM), 0.02),
        "dt_bias": dt0 + jnp.log(-jnp.expm1(-dt0)),
        "a_log": jnp.log(jax.random.uniform(ks[13], (L, SSM_HEADS), f32, 1.0, 16.0)),
        "d_skip": 1.0 + nrm(ks[14], (L, SSM_HEADS), 0.02),
        "ssm_norm_w": 1.0 + nrm(ks[15], (L, D_INNER), 0.02),
        "w_attn_proj": nrm(ks[16], (L, ATTN_WIDTH, D_MODEL), ATTN_WIDTH ** -0.5),
        "w_ssm_proj": nrm(ks[17], (L, D_INNER, D_MODEL), D_INNER ** -0.5),
        "w_out": nrm(ks[18], (L, D_MODEL, D_MODEL), D_MODEL ** -0.5),
        "norm2_w": 1.0 + nrm(ks[19], (L, D_MODEL), 0.02),
        "w_router": nrm(ks[20], (L, D_MODEL, N_EXPERTS), D_MODEL ** -0.5),
        "b_router": nrm(ks[21], (L, N_EXPERTS), 0.01),
        "w_gate_up": nrm(ks[22], (L, N_EXPERTS, D_MODEL, 2 * D_FF), D_MODEL ** -0.5),
        "b_gate_up": nrm(ks[23], (L, N_EXPERTS, 2 * D_FF), 0.02),
        "w_down": nrm(ks[24], (L, N_EXPERTS, D_FF, D_MODEL), D_FF ** -0.5),
        "b_down": nrm(ks[25], (L, N_EXPERTS, D_MODEL), 0.02),
    }


def reference(x, norm1_w, w_in, q_norm_w, k_norm_w, lambda_q1, lambda_k1, lambda_q2, lambda_k2,
              subln_w, conv_w, conv_b, dt_bias, a_log, d_skip, ssm_norm_w, w_attn_proj, w_ssm_proj,
              w_out, norm2_w, w_router, b_router, w_gate_up, b_gate_up, w_down, b_down):
    h = x
    for l in range(DEPTH):
        lambda_init = 0.8 - 0.6 * math.exp(-0.3 * l)
        u = _rms_norm(h, norm1_w[l], EPS)
        proj = u @ w_in[l]
        q, k, v, z, xbc, dt_raw, gate_logits = jnp.split(proj, _split_points(), axis=-1)
        attn = _diff_attention(q, k, v, q_norm_w[l], k_norm_w[l], lambda_q1[l], lambda_k1[l],
                               lambda_q2[l], lambda_k2[l], subln_w[l], lambda_init)
        ssm = _ssd_branch(z, xbc, dt_raw, conv_w[l], conv_b[l], dt_bias[l], a_log[l],
                          d_skip[l], ssm_norm_w[l])
        g_attn, g_ssm = jnp.split(jax.nn.sigmoid(gate_logits), N_BRANCHES, axis=-1)
        mixed = g_attn * (attn @ w_attn_proj[l]) + g_ssm * (ssm @ w_ssm_proj[l])
        h = h + mixed @ w_out[l]
        h = h + _moe(_rms_norm(h, norm2_w[l], EPS), w_router[l], b_router[l], w_gate_up[l],
                     b_gate_up[l], w_down[l], b_down[l])
    return h
```

```python
import functools
import math

import jax
import jax.numpy as jnp
from jax import lax
from jax.experimental import pallas as pl
from jax.experimental.pallas import tpu as pltpu

F32, BF16, I32 = jnp.float32, jnp.bfloat16, jnp.int32

D_MODEL = 1024
ATTN_HEADS = 8
ATTN_HEAD_DIM = 64
ATTN_WIDTH = ATTN_HEADS * 2 * ATTN_HEAD_DIM
SSM_EXPAND = 2
D_INNER = SSM_EXPAND * D_MODEL
SSM_HEAD_DIM = 64
SSM_HEADS = D_INNER // SSM_HEAD_DIM
SSM_GROUPS = 4
SSM_STATE = 128
CONV_WIDTH = 4
CONV_DIM = D_INNER + 2 * SSM_GROUPS * SSM_STATE
SSM_CHUNK = 128
N_EXPERTS = 32
TOP_K = 4
D_FF = D_MODEL
SWIGLU_LIMIT = 7.0
SWIGLU_ALPHA = 1.702
EPS = 1e-5
QK_EPS = 1e-6

LANES = 128
HEADS_PER_GROUP = SSM_HEADS // SSM_GROUPS
GROUP_WIDTH = D_INNER // SSM_GROUPS
LOG2E = math.log2(math.e)
NEG = -1e30
VMEM_LIMIT = 56 * 1024 * 1024

COL_Q, COL_K, COL_V = 0, ATTN_WIDTH, 2 * ATTN_WIDTH
COL_XBC = 3 * ATTN_WIDTH
COL_Z = COL_XBC + CONV_DIM
COL_GATE = COL_Z + D_INNER
PROJ_W = COL_GATE + 2 * D_MODEL

ATTN_TILE = 256
MOE_TILE = 512


def _params(*sem):
    return pltpu.CompilerParams(dimension_semantics=sem, vmem_limit_bytes=VMEM_LIMIT)


def _sigmoid(x):
    return 1.0 / (1.0 + jnp.exp(-x))


def _inproj_body(x_ref, g_ref, w_ref, wdt_ref, o_ref, dt_ref, u_scr):
    @pl.when(pl.program_id(1) == 0)
    def _():
        x = x_ref[...]
        u = x * lax.rsqrt(jnp.mean(x * x, axis=-1, keepdims=True) + EPS) * g_ref[...]
        ub = u.astype(BF16)
        u_scr[...] = ub
        dt_ref[...] = jnp.dot(ub, wdt_ref[...], preferred_element_type=F32)

    o_ref[...] = jnp.dot(u_scr[...], w_ref[...], preferred_element_type=F32).astype(o_ref.dtype)


def _in_proj(xf, gain, w_main, w_dt):
    T = xf.shape[0]
    tm = min(1024, T)
    tn = 1024
    return pl.pallas_call(
        _inproj_body,
        grid=(T // tm, PROJ_W // tn),
        in_specs=[
            pl.BlockSpec((tm, D_MODEL), lambda i, j: (i, 0)),
            pl.BlockSpec((1, D_MODEL), lambda i, j: (0, 0)),
            pl.BlockSpec((D_MODEL, tn), lambda i, j: (0, j)),
            pl.BlockSpec((D_MODEL, LANES), lambda i, j: (0, 0)),
        ],
        out_specs=[
            pl.BlockSpec((tm, tn), lambda i, j: (i, j)),
            pl.BlockSpec((tm, LANES), lambda i, j: (i, 0)),
        ],
        out_shape=[
            jax.ShapeDtypeStruct((T, PROJ_W), BF16),
            jax.ShapeDtypeStruct((T, LANES), F32),
        ],
        scratch_shapes=[pltpu.VMEM((tm, D_MODEL), BF16)],
        compiler_params=_params("parallel", "arbitrary"),
        name="in_proj",
    )(xf, gain, w_main, w_dt)


def _split3(x):
    hi = x.astype(BF16).astype(F32)
    r = x - hi
    mid = r.astype(BF16).astype(F32)
    return hi, mid, r - mid


def _attn_prep_body(q_ref, k_ref, v_ref, qg_ref, kg_ref, c_ref, qT_ref, ka_ref, vT_ref, *, tk):
    ts = q_ref.shape[0]
    lane = lax.broadcasted_iota(I32, (ts, LANES), 1)
    lo_half = lane < ATTN_HEAD_DIM

    def half_norm(x, g):
        x2 = x * x
        s_lo = jnp.sum(jnp.where(lo_half, x2, 0.0), axis=-1, keepdims=True)
        s_hi = jnp.sum(jnp.where(lo_half, 0.0, x2), axis=-1, keepdims=True)
        ms = jnp.where(lo_half, s_lo, s_hi) * (1.0 / ATTN_HEAD_DIM)
        return x * lax.rsqrt(ms + QK_EPS) * g

    q = half_norm(q_ref[...].astype(F32), qg_ref[...])
    k = half_norm(k_ref[...].astype(F32), kg_ref[...])

    row = lax.broadcasted_iota(I32, (ts, LANES), 0) + pl.program_id(2) * ts
    bias = c_ref[...] * (row % tk).astype(F32)
    b_hi, b_mid, b_lo = _split3(bias)
    k_tail = jnp.where(lane == ATTN_HEAD_DIM, b_hi,
                       jnp.where(lane == ATTN_HEAD_DIM + 1, b_mid,
                                 jnp.where(lane == ATTN_HEAD_DIM + 2, b_lo, 0.0)))
    q_tail = jnp.where(lane < ATTN_HEAD_DIM + 3, 1.0, 0.0)

    for comp in range(2):
        qc = q if comp == 0 else pltpu.roll(q, ATTN_HEAD_DIM, 1)
        kc = k if comp == 0 else pltpu.roll(k, ATTN_HEAD_DIM, 1)
        qT_ref[comp] = jnp.where(lo_half, qc, q_tail).T.astype(BF16)
        ka_ref[comp] = jnp.where(lo_half, kc, k_tail).astype(BF16)
    vT_ref[...] = v_ref[...].astype(F32).T.astype(BF16)


def _attn_prep(proj, qg, kg, c_tab, B, S):
    ts = min(512, S)
    ns = S // ts
    H = ATTN_HEADS
    hb = ATTN_WIDTH // LANES
    return pl.pallas_call(
        functools.partial(_attn_prep_body, tk=ATTN_TILE),
        grid=(B, H, ns),
        in_specs=[
            pl.BlockSpec((ts, LANES), lambda b, h, s: (b * ns + s, h)),
            pl.BlockSpec((ts, LANES), lambda b, h, s: (b * ns + s, hb + h)),
            pl.BlockSpec((ts, LANES), lambda b, h, s: (b * ns + s, 2 * hb + h)),
            pl.BlockSpec((1, LANES), lambda b, h, s: (0, 0)),
            pl.BlockSpec((1, LANES), lambda b, h, s: (0, 0)),
            pl.BlockSpec((None, 1, LANES), lambda b, h, s: (h, 0, 0)),
        ],
        out_specs=[
            pl.BlockSpec((None, None, 2, LANES, ts), lambda b, h, s: (b, h, 0, 0, s)),
            pl.BlockSpec((None, None, 2, ts, LANES), lambda b, h, s: (b, h, 0, s, 0)),
            pl.BlockSpec((None, None, LANES, ts), lambda b, h, s: (b, h, 0, s)),
        ],
        out_shape=[
            jax.ShapeDtypeStruct((B, H, 2, LANES, S), BF16),
            jax.ShapeDtypeStruct((B, H, 2, S, LANES), BF16),
            jax.ShapeDtypeStruct((B, H, LANES, S), BF16),
        ],
        compiler_params=_params("parallel", "parallel", "parallel"),
        name="attn_prep",
    )(proj, proj, proj, qg, kg, c_tab)


def _attn_body(lam_ref, qT_ref, ka_ref, vT_ref, c_ref, w_ref, o_ref, acc_ref, m_ref, l_ref,
               *, tile, out_scale):
    i = pl.program_id(2)
    acc_ref[...] = jnp.zeros_like(acc_ref)
    m_ref[...] = jnp.full_like(m_ref, NEG)
    l_ref[...] = jnp.zeros_like(l_ref)
    c_row = c_ref[...]

    def step(j, masked):
        start = pl.multiple_of(j * tile, tile)
        cb = c_row * ((j - i) * tile).astype(F32)
        for comp in range(2):
            sT = jnp.dot(ka_ref[comp, pl.ds(start, tile), :], qT_ref[comp],
                         preferred_element_type=F32)
            if masked:
                key = lax.broadcasted_iota(I32, sT.shape, 0)
                qry = lax.broadcasted_iota(I32, sT.shape, 1)
                sT = jnp.where(key <= qry, sT, NEG)
            m_old = m_ref[comp]
            m_new = jnp.maximum(m_old, jnp.max(sT, axis=0, keepdims=True) + cb)
            alpha = jnp.exp2(m_old - m_new)
            p = jnp.exp2(sT - (m_new - cb))
            l_ref[comp] = alpha * l_ref[comp] + jnp.sum(p, axis=0, keepdims=True)
            pv = jnp.dot(vT_ref[:, pl.ds(start, tile)], p.astype(BF16),
                         preferred_element_type=F32)
            acc_ref[comp] = alpha * acc_ref[comp] + pv
            m_ref[comp] = m_new

    def full_step(j, carry):
        step(j, False)
        return carry

    lax.fori_loop(0, i, full_step, 0)
    step(i, True)

    o = acc_ref[0] / l_ref[0] - lam_ref[0] * (acc_ref[1] / l_ref[1])
    ms = jnp.mean(o * o, axis=0, keepdims=True)
    y = o * lax.rsqrt(ms + EPS) * w_ref[...] * out_scale
    o_ref[...] = y.T.astype(o_ref.dtype)


def _attention(lam, qT, ka, vT, c_row_tab, subln_col, B, S, lambda_init):
    tile = ATTN_TILE
    H = ATTN_HEADS
    nq = S // tile
    return pl.pallas_call(
        functools.partial(_attn_body, tile=tile, out_scale=1.0 - lambda_init),
        grid=(B, H, nq),
        in_specs=[
            pl.BlockSpec(memory_space=pltpu.SMEM),
            pl.BlockSpec((None, None, 2, LANES, tile), lambda b, h, i: (b, h, 0, 0, i)),
            pl.BlockSpec((None, None, 2, S, LANES), lambda b, h, i: (b, h, 0, 0, 0)),
            pl.BlockSpec((None, None, LANES, S), lambda b, h, i: (b, h, 0, 0)),
            pl.BlockSpec((None, 1, tile), lambda b, h, i: (h, 0, 0)),
            pl.BlockSpec((LANES, 1), lambda b, h, i: (0, 0)),
        ],
        out_specs=pl.BlockSpec((tile, LANES), lambda b, h, i: (b * nq + i, h)),
        out_shape=jax.ShapeDtypeStruct((B * S, ATTN_WIDTH), BF16),
        scratch_shapes=[
            pltpu.VMEM((2, LANES, tile), F32),
            pltpu.VMEM((2, 1, tile), F32),
            pltpu.VMEM((2, 1, tile), F32),
        ],
        compiler_params=_params("parallel", "parallel", "arbitrary"),
        name="diff_attn",
    )(lam, qT, ka, vT, c_row_tab, subln_col)


def _expand_heads(x, e_ref):
    lane = lax.broadcasted_iota(I32, x.shape, 1)
    hi, mid, lo = _split3(x)
    parts = jnp.where(lane < SSM_HEADS, hi,
                      jnp.where(lane < 2 * SSM_HEADS, mid,
                                jnp.where(lane < 3 * SSM_HEADS, lo, 0.0)))
    return jnp.dot(parts.astype(BF16), e_ref[...], preferred_element_type=F32)


def _ssd_body(xbc_ref, z_ref, dt_ref, cw_ref, cb_ref, dtb_ref, alog_ref, dskip_ref, nw_ref, e_ref,
              o_ref, ext_ref, state_ref, y_ref):
    Q = SSM_CHUNK
    halo = 8

    @pl.when(pl.program_id(1) == 0)
    def _():
        ext_ref[0:halo, :] = jnp.zeros((halo, CONV_DIM), F32)
        state_ref[...] = jnp.zeros_like(state_ref)

    xr = xbc_ref[...].astype(F32)
    ext_ref[halo:halo + Q, :] = xr
    conv = cb_ref[...] + cw_ref[CONV_WIDTH - 1:CONV_WIDTH, :] * xr
    for w in range(CONV_WIDTH - 1):
        off = halo - (CONV_WIDTH - 1) + w
        conv = conv + cw_ref[w:w + 1, :] * ext_ref[off:off + Q, :]
    ext_ref[0:halo, :] = xr[Q - halo:Q, :]
    xc = conv * _sigmoid(conv)
    xs = xc[:, :D_INNER]

    raw = dt_ref[...] + dtb_ref[...]
    dt = jnp.maximum(raw, 0.0) + jnp.log(1.0 + jnp.exp(-jnp.abs(raw)))
    dA = dt * (-jnp.exp(alog_ref[...]))
    t_idx = lax.broadcasted_iota(I32, (Q, Q), 0)
    s_idx = lax.broadcasted_iota(I32, (Q, Q), 1)
    causal = s_idx <= t_idx
    cum = jnp.dot(causal.astype(F32), dA, preferred_element_type=F32,
                  precision=lax.Precision.HIGHEST)
    cumT = cum.T
    ecum = jnp.exp(cum)
    decay = jnp.exp(cum[Q - 1:Q, :] - cum)

    dt_e = _expand_heads(dt, e_ref)
    ecum_e = _expand_heads(ecum, e_ref)
    decay_e = _expand_heads(decay, e_ref)
    xdt = xs * dt_e
    xdt_b = xdt.astype(BF16)
    xdec_b = (xdt * decay_e).astype(BF16)

    lane = lax.broadcasted_iota(I32, (Q, LANES), 1)
    first_head = lane < SSM_HEAD_DIM
    for g in range(SSM_GROUPS):
        bcol = D_INNER + g * SSM_STATE
        ccol = D_INNER + SSM_GROUPS * SSM_STATE + g * SSM_STATE
        Bg = xc[:, bcol:bcol + SSM_STATE].astype(BF16)
        Cg = xc[:, ccol:ccol + SSM_STATE].astype(BF16)
        gs = slice(g * GROUP_WIDTH, (g + 1) * GROUP_WIDTH)
        CB = lax.dot_general(Cg, Bg, (((1,), (1,)), ((), ())), preferred_element_type=F32)
        state = state_ref[g]
        y_off = jnp.dot(Cg, state.astype(BF16), preferred_element_type=F32) * ecum_e[:, gs]
        for pair in range(HEADS_PER_GROUP // 2):
            h0 = g * HEADS_PER_GROUP + 2 * pair
            ms = []
            for h in (h0, h0 + 1):
                diff = cum[:, h:h + 1] - cumT[h:h + 1, :]
                L = jnp.exp(jnp.where(causal, diff, NEG))
                ms.append((CB * L).astype(BF16))
            lhs = jnp.concatenate(ms, axis=1)
            c0 = h0 * SSM_HEAD_DIM
            xp = xdt_b[:, c0:c0 + LANES]
            zero = jnp.zeros_like(xp)
            rhs = jnp.concatenate([jnp.where(first_head, xp, zero),
                                   jnp.where(first_head, zero, xp)], axis=0)
            y_ref[:, c0:c0 + LANES] = (jnp.dot(lhs, rhs, preferred_element_type=F32)
                                       + y_off[:, c0 - g * GROUP_WIDTH:c0 - g * GROUP_WIDTH + LANES])
        upd = lax.dot_general(Bg, xdec_b[:, gs], (((0,), (0,)), ((), ())),
                              preferred_element_type=F32)
        state_ref[g] = state * ecum_e[Q - 1:Q, gs] + upd

    y = y_ref[...] + xs * dskip_ref[...]
    z = z_ref[...].astype(F32)
    y = y * (z * _sigmoid(z))
    for g in range(SSM_GROUPS):
        gs = slice(g * GROUP_WIDTH, (g + 1) * GROUP_WIDTH)
        yg = y[:, gs]
        ms = jnp.mean(yg * yg, axis=-1, keepdims=True)
        o_ref[:, gs] = (yg * lax.rsqrt(ms + EPS) * nw_ref[:, gs]).astype(o_ref.dtype)


def _ssd(proj, dt_raw, conv_w, conv_b, dtb3, alog3, dskip_e, norm_w, expand, B, S):
    Q = SSM_CHUNK
    nc = S // Q
    row = lambda b, c: b * nc + c
    const = lambda b, c: (0, 0)
    return pl.pallas_call(
        _ssd_body,
        grid=(B, nc),
        in_specs=[
            pl.BlockSpec((Q, CONV_DIM), lambda b, c: (row(b, c), COL_XBC // CONV_DIM)),
            pl.BlockSpec((Q, D_INNER), lambda b, c: (row(b, c), COL_Z // D_INNER)),
            pl.BlockSpec((Q, LANES), lambda b, c: (row(b, c), 0)),
            pl.BlockSpec((CONV_WIDTH, CONV_DIM), const),
            pl.BlockSpec((1, CONV_DIM), const),
            pl.BlockSpec((1, LANES), const),
            pl.BlockSpec((1, LANES), const),
            pl.BlockSpec((1, D_INNER), const),
            pl.BlockSpec((1, D_INNER), const),
            pl.BlockSpec((LANES, D_INNER), const),
        ],
        out_specs=pl.BlockSpec((Q, D_INNER), lambda b, c: (row(b, c), 0)),
        out_shape=jax.ShapeDtypeStruct((B * S, D_INNER), BF16),
        scratch_shapes=[
            pltpu.VMEM((Q + 8, CONV_DIM), F32),
            pltpu.VMEM((SSM_GROUPS, SSM_STATE, GROUP_WIDTH), F32),
            pltpu.VMEM((Q, D_INNER), F32),
        ],
        compiler_params=_params("parallel", "arbitrary"),
        name="ssd",
    )(proj, proj, dt_raw, conv_w, conv_b, dtb3, alog3, dskip_e, norm_w, expand)


def _merge_body(attn_ref, ssm_ref, gate_ref, x_ref, wa_ref, ws_ref, wo_ref, n2_ref, wr_ref, br_ref,
                h1_ref, hn_ref, lg_ref):
    gate = gate_ref[...].astype(F32)
    ya = jnp.dot(attn_ref[...], wa_ref[...], preferred_element_type=F32)
    ys = jnp.dot(ssm_ref[...], ws_ref[...], preferred_element_type=F32)
    mixed = _sigmoid(gate[:, :D_MODEL]) * ya + _sigmoid(gate[:, D_MODEL:]) * ys
    h1 = x_ref[...] + jnp.dot(mixed.astype(BF16), wo_ref[...], preferred_element_type=F32)
    h1_ref[...] = h1
    hn = h1 * lax.rsqrt(jnp.mean(h1 * h1, axis=-1, keepdims=True) + EPS) * n2_ref[...]
    hn_ref[...] = hn
    lg_ref[...] = jnp.dot(hn.astype(BF16), wr_ref[...], preferred_element_type=F32) + br_ref[...]


def _merge(attn, ssm, proj, xf, wa, ws, wo, n2, wr, br):
    T = xf.shape[0]
    tm = min(256, T)
    const = lambda i: (0, 0)
    return pl.pallas_call(
        _merge_body,
        grid=(T // tm,),
        in_specs=[
            pl.BlockSpec((tm, ATTN_WIDTH), lambda i: (i, 0)),
            pl.BlockSpec((tm, D_INNER), lambda i: (i, 0)),
            pl.BlockSpec((tm, 2 * D_MODEL), lambda i: (i, COL_GATE // (2 * D_MODEL))),
            pl.BlockSpec((tm, D_MODEL), lambda i: (i, 0)),
            pl.BlockSpec((ATTN_WIDTH, D_MODEL), const),
            pl.BlockSpec((D_INNER, D_MODEL), const),
            pl.BlockSpec((D_MODEL, D_MODEL), const),
            pl.BlockSpec((1, D_MODEL), const),
            pl.BlockSpec((D_MODEL, LANES), const),
            pl.BlockSpec((1, LANES), const),
        ],
        out_specs=[
            pl.BlockSpec((tm, D_MODEL), lambda i: (i, 0)),
            pl.BlockSpec((tm, D_MODEL), lambda i: (i, 0)),
            pl.BlockSpec((tm, LANES), lambda i: (i, 0)),
        ],
        out_shape=[
            jax.ShapeDtypeStruct((T, D_MODEL), F32),
            jax.ShapeDtypeStruct((T, D_MODEL), F32),
            jax.ShapeDtypeStruct((T, LANES), F32),
        ],
        compiler_params=_params("parallel"),
        name="merge",
    )(attn, ssm, proj, xf, wa, ws, wo, n2, wr, br)


def _route_body(lg_ref, idx_ref, w_ref, rank_ref, cnt_ref, base_ref):
    tm = lg_ref.shape[0]

    @pl.when(pl.program_id(0) == 0)
    def _():
        base_ref[...] = jnp.zeros_like(base_ref)

    logit = lg_ref[...].T[:N_EXPERTS, :]
    eio = lax.broadcasted_iota(I32, logit.shape, 0)
    vals, hits = [], []
    for k in range(TOP_K):
        mx = jnp.max(logit, axis=0, keepdims=True)
        idx = jnp.min(jnp.where(logit == mx, eio, N_EXPERTS), axis=0, keepdims=True)
        hit = eio == idx
        logit = jnp.where(hit, -jnp.inf, logit)
        idx_ref[k:k + 1, :] = idx
        vals.append(mx)
        hits.append(hit)
    exps = [jnp.exp(v - vals[0]) for v in vals]
    denom = exps[0] + exps[1] + exps[2] + exps[3]
    for k in range(TOP_K):
        w_ref[k:k + 1, :] = exps[k] / denom

    sel = hits[0] | hits[1] | hits[2] | hits[3]
    before = (lax.broadcasted_iota(I32, (tm, tm), 0) < lax.broadcasted_iota(I32, (tm, tm), 1))
    prefix = jnp.dot(sel.astype(BF16), before.astype(BF16), preferred_element_type=F32)
    base = base_ref[:, 0:1]
    offs = prefix + base
    for k in range(TOP_K):
        rank_ref[k:k + 1, :] = jnp.sum(jnp.where(hits[k], offs, 0.0), axis=0,
                                       keepdims=True).astype(I32)
    new_base = base + jnp.sum(sel.astype(F32), axis=1, keepdims=True)
    base_ref[...] = jnp.broadcast_to(new_base, base_ref.shape)
    cnt_ref[...] = jnp.broadcast_to(new_base, cnt_ref.shape)


def _route(logits):
    T = logits.shape[0]
    tm = min(512, T)
    return pl.pallas_call(
        _route_body,
        grid=(T // tm,),
        in_specs=[pl.BlockSpec((tm, LANES), lambda i: (i, 0))],
        out_specs=[
            pl.BlockSpec((TOP_K, tm), lambda i: (0, i)),
            pl.BlockSpec((TOP_K, tm), lambda i: (0, i)),
            pl.BlockSpec((TOP_K, tm), lambda i: (0, i)),
            pl.BlockSpec((N_EXPERTS, LANES), lambda i: (0, 0)),
        ],
        out_shape=[
            jax.ShapeDtypeStruct((TOP_K, T), I32),
            jax.ShapeDtypeStruct((TOP_K, T), F32),
            jax.ShapeDtypeStruct((TOP_K, T), I32),
            jax.ShapeDtypeStruct((N_EXPERTS, LANES), F32),
        ],
        scratch_shapes=[pltpu.VMEM((N_EXPERTS, LANES), F32)],
        compiler_params=_params("arbitrary"),
        name="route",
    )(logits)


def _row_copy(src_ref, src_row, dst_ref, dst_row, sem):
    return pltpu.make_async_copy(src_ref.at[pl.ds(src_row, 1), :],
                                 dst_ref.at[pl.ds(dst_row, 1), :], sem)


def _dispatch_body(pos_ref, hn_ref, xs_in_ref, xs_ref, sem):
    del xs_in_ref
    tm = hn_ref.shape[0]

    def issue(r, carry):
        for k in range(TOP_K):
            _row_copy(hn_ref, r, xs_ref, pos_ref[r * TOP_K + k], sem).start()
        return carry

    def drain(r, carry):
        for k in range(TOP_K):
            _row_copy(hn_ref, r, xs_ref, pos_ref[r * TOP_K + k], sem).wait()
        return carry

    lax.fori_loop(0, tm, issue, 0)
    lax.fori_loop(0, tm, drain, 0)


def _dispatch(pos_flat, hn, xs_init):
    T = hn.shape[0]
    tm = min(256, T)
    return pl.pallas_call(
        _dispatch_body,
        grid=(T // tm,),
        in_specs=[
            pl.BlockSpec((tm * TOP_K,), lambda i: (i,), memory_space=pltpu.SMEM),
            pl.BlockSpec((tm, D_MODEL), lambda i: (i, 0)),
            pl.BlockSpec(memory_space=pl.ANY),
        ],
        out_specs=pl.BlockSpec(memory_space=pl.ANY),
        out_shape=jax.ShapeDtypeStruct(xs_init.shape, xs_init.dtype),
        scratch_shapes=[pltpu.SemaphoreType.DMA(())],
        input_output_aliases={2: 0},
        compiler_params=_params("arbitrary"),
        name="dispatch",
    )(pos_flat, hn, xs_init)


def _ffn_body(te_ref, nu_ref, xs_ref, wgu_ref, bgu_ref, wd_ref, bd_ref, ys_ref):
    del te_ref

    @pl.when(pl.program_id(0) < nu_ref[0])
    def _():
        gu = jnp.dot(xs_ref[...].astype(BF16), wgu_ref[...], preferred_element_type=F32) + bgu_ref[...]
        glu = jnp.minimum(gu[:, :D_FF], SWIGLU_LIMIT)
        lin = jnp.clip(gu[:, D_FF:], -SWIGLU_LIMIT, SWIGLU_LIMIT)
        act = glu * _sigmoid(SWIGLU_ALPHA * glu) * (lin + 1.0)
        ys_ref[...] = jnp.dot(act.astype(BF16), wd_ref[...], preferred_element_type=F32) + bd_ref[...]


def _ffn(tile_expert, n_used, xs, wgu, bgu, wd, bd):
    M_pad = xs.shape[0]
    tm = MOE_TILE
    n_tiles = M_pad // tm
    row = lambda i, te, nu: (jnp.minimum(i, nu[0] - 1), 0)
    return pl.pallas_call(
        _ffn_body,
        grid_spec=pltpu.PrefetchScalarGridSpec(
            num_scalar_prefetch=2,
            grid=(n_tiles,),
            in_specs=[
                pl.BlockSpec((tm, D_MODEL), row),
                pl.BlockSpec((None, D_MODEL, 2 * D_FF), lambda i, te, nu: (te[i], 0, 0)),
                pl.BlockSpec((None, 1, 2 * D_FF), lambda i, te, nu: (te[i], 0, 0)),
                pl.BlockSpec((None, D_FF, D_MODEL), lambda i, te, nu: (te[i], 0, 0)),
                pl.BlockSpec((None, 1, D_MODEL), lambda i, te, nu: (te[i], 0, 0)),
            ],
            out_specs=pl.BlockSpec((tm, D_MODEL), row),
        ),
        out_shape=jax.ShapeDtypeStruct((M_pad, D_MODEL), F32),
        compiler_params=_params("arbitrary"),
        name="expert_ffn",
    )(tile_expert, n_used, xs, wgu, bgu, wd, bd)


def _combine_body(pos_ref, w_ref, h1_ref, ys_ref, o_ref, buf_ref, sem):
    tm = h1_ref.shape[0]

    def issue(r, carry):
        for k in range(TOP_K):
            _row_copy(ys_ref, pos_ref[r * TOP_K + k], buf_ref.at[k], r, sem).start()
        return carry

    def drain(r, carry):
        for k in range(TOP_K):
            _row_copy(ys_ref, pos_ref[r * TOP_K + k], buf_ref.at[k], r, sem).wait()
        return carry

    lax.fori_loop(0, tm, issue, 0)
    lax.fori_loop(0, tm, drain, 0)
    w = w_ref[...]
    out = h1_ref[...]
    for k in range(TOP_K):
        out = out + w[:, k:k + 1] * buf_ref[k]
    o_ref[...] = out


def _combine(pos_flat, w_tok, h1, ys):
    T = h1.shape[0]
    tm = min(256, T)
    return pl.pallas_call(
        _combine_body,
        grid=(T // tm,),
        in_specs=[
            pl.BlockSpec((tm * TOP_K,), lambda i: (i,), memory_space=pltpu.SMEM),
            pl.BlockSpec((tm, TOP_K), lambda i: (i, 0)),
            pl.BlockSpec((tm, D_MODEL), lambda i: (i, 0)),
            pl.BlockSpec(memory_space=pl.ANY),
        ],
        out_specs=pl.BlockSpec((tm, D_MODEL), lambda i: (i, 0)),
        out_shape=jax.ShapeDtypeStruct((T, D_MODEL), F32),
        scratch_shapes=[pltpu.VMEM((TOP_K, tm, D_MODEL), F32), pltpu.SemaphoreType.DMA(())],
        compiler_params=_params("arbitrary"),
        name="combine",
    )(pos_flat, w_tok, h1, ys)


def _layer(h, l, p):
    B, S, _ = h.shape
    T = B * S
    lambda_init = 0.8 - 0.6 * math.exp(-0.3 * l)
    hf = h.reshape(T, D_MODEL)

    w_in = p["w_in"]
    sizes = [ATTN_WIDTH, ATTN_WIDTH, ATTN_WIDTH, D_INNER, CONV_DIM, SSM_HEADS, 2 * D_MODEL]
    offs = [0]
    for s in sizes:
        offs.append(offs[-1] + s)
    wq, wk, wv, wz, wxbc, wdt, wg = (w_in[:, offs[n]:offs[n + 1]] for n in range(7))
    w_main = jnp.concatenate([wq, wk, wv, wxbc, wz, wg], axis=1).astype(BF16)
    pad = jnp.zeros((D_MODEL, LANES - 3 * SSM_HEADS), F32)
    w_dt = jnp.concatenate([wdt, wdt, wdt, pad], axis=1).astype(BF16)

    proj, dt_raw = _in_proj(hf, p["norm1_w"].reshape(1, D_MODEL), w_main, w_dt)

    qk_scale = ATTN_HEAD_DIM ** -0.5 * LOG2E
    qg = (jnp.tile(p["q_norm_w"], 2) * qk_scale).reshape(1, LANES)
    kg = jnp.tile(p["k_norm_w"], 2).reshape(1, LANES)
    slopes = jnp.power(2.0, -8.0 * (jnp.arange(ATTN_HEADS, dtype=F32) + 1.0) / ATTN_HEADS) * LOG2E
    c_tab = jnp.broadcast_to(slopes[:, None, None], (ATTN_HEADS, 1, LANES))
    c_row_tab = jnp.broadcast_to(slopes[:, None, None], (ATTN_HEADS, 1, ATTN_TILE))
    lam = (jnp.exp(jnp.sum(p["lambda_q1"] * p["lambda_k1"]))
           - jnp.exp(jnp.sum(p["lambda_q2"] * p["lambda_k2"])) + lambda_init).reshape(1)
    qT, ka, vT = _attn_prep(proj, qg, kg, c_tab, B, S)
    attn = _attention(lam, qT, ka, vT, c_row_tab, p["subln_w"].reshape(LANES, 1), B, S, lambda_init)

    rep3 = lambda v: jnp.concatenate([v, v, v, jnp.zeros((LANES - 3 * SSM_HEADS,), F32)]).reshape(1, LANES)
    head_of_row = jnp.arange(LANES) % SSM_HEADS
    head_of_col = jnp.arange(D_INNER) // SSM_HEAD_DIM
    expand = ((head_of_row[:, None] == head_of_col[None, :])
              & (jnp.arange(LANES)[:, None] < 3 * SSM_HEADS)).astype(BF16)
    ssm = _ssd(proj, dt_raw, p["conv_w"], p["conv_b"].reshape(1, CONV_DIM), rep3(p["dt_bias"]),
               rep3(p["a_log"]), jnp.repeat(p["d_skip"], SSM_HEAD_DIM).reshape(1, D_INNER),
               p["ssm_norm_w"].reshape(1, D_INNER), expand, B, S)

    wr = jnp.pad(p["w_router"], ((0, 0), (0, LANES - N_EXPERTS))).astype(BF16)
    br = jnp.pad(p["b_router"], (0, LANES - N_EXPERTS)).reshape(1, LANES)
    h1, hn, logits = _merge(attn, ssm, proj, hf, p["w_attn_proj"].astype(BF16),
                            p["w_ssm_proj"].astype(BF16), p["w_out"].astype(BF16),
                            p["norm2_w"].reshape(1, D_MODEL), wr, br)

    idx, w_top, rank, cnt = _route(logits)
    counts = cnt[:, 0].astype(I32)
    tiles_per = (counts + MOE_TILE - 1) // MOE_TILE
    tile_end = jnp.cumsum(tiles_per)
    group_start = (tile_end - tiles_per) * MOE_TILE
    M_pad = T * TOP_K + N_EXPERTS * MOE_TILE
    n_tiles = M_pad // MOE_TILE
    tile_expert = jnp.minimum(jnp.searchsorted(tile_end, jnp.arange(n_tiles), side="right"),
                              N_EXPERTS - 1).astype(I32)
    n_used = tile_end[-1:].astype(I32)
    pos = jnp.take(group_start, idx) + rank
    pos_flat = pos.T.reshape(T * TOP_K)

    xs = _dispatch(pos_flat, hn, jnp.zeros((M_pad, D_MODEL), F32))
    ys = _ffn(tile_expert, n_used, xs, p["w_gate_up"].astype(BF16),
              p["b_gate_up"].reshape(N_EXPERTS, 1, 2 * D_FF), p["w_down"].astype(BF16),
              p["b_down"].reshape(N_EXPERTS, 1, D_MODEL))
    out = _combine(pos_flat, w_top.T, h1, ys)
    return out.reshape(B, S, D_MODEL)


def kernel(x, norm1_w, w_in, q_norm_w, k_norm_w, lambda_q1, lambda_k1, lambda_q2, lambda_k2, subln_w, conv_w, conv_b, dt_bias, a_log, d_skip, ssm_norm_w, w_attn_proj, w_ssm_proj, w_out, norm2_w, w_router, b_router, w_gate_up, b_gate_up, w_down, b_down):
    params = dict(norm1_w=norm1_w, w_in=w_in, q_norm_w=q_norm_w, k_norm_w=k_norm_w,
                  lambda_q1=lambda_q1, lambda_k1=lambda_k1, lambda_q2=lambda_q2, lambda_k2=lambda_k2,
                  subln_w=subln_w, conv_w=conv_w, conv_b=conv_b, dt_bias=dt_bias, a_log=a_log,
                  d_skip=d_skip, ssm_norm_w=ssm_norm_w, w_attn_proj=w_attn_proj,
                  w_ssm_proj=w_ssm_proj, w_out=w_out, norm2_w=norm2_w, w_router=w_router,
                  b_router=b_router, w_gate_up=w_gate_up, b_gate_up=b_gate_up, w_down=w_down,
                  b_down=b_down)
    h = x
    for l in range(w_in.shape[0]):
        h = _layer(h, l, {k: v[l] for k, v in params.items()})
    return h
```

```python
import functools
import math

import jax
import jax.numpy as jnp
from jax import lax
from jax.experimental import pallas as pl
from jax.experimental.pallas import tpu as pltpu

F32, BF16, I32 = jnp.float32, jnp.bfloat16, jnp.int32

D_MODEL = 1024
ATTN_HEADS = 8
ATTN_HEAD_DIM = 64
ATTN_WIDTH = ATTN_HEADS * 2 * ATTN_HEAD_DIM
SSM_EXPAND = 2
D_INNER = SSM_EXPAND * D_MODEL
SSM_HEAD_DIM = 64
SSM_HEADS = D_INNER // SSM_HEAD_DIM
SSM_GROUPS = 4
SSM_STATE = 128
CONV_WIDTH = 4
CONV_DIM = D_INNER + 2 * SSM_GROUPS * SSM_STATE
SSM_CHUNK = 128
N_EXPERTS = 32
TOP_K = 4
D_FF = D_MODEL
SWIGLU_LIMIT = 7.0
SWIGLU_ALPHA = 1.702
EPS = 1e-5
QK_EPS = 1e-6

LANES = 128
HEADS_PER_GROUP = SSM_HEADS // SSM_GROUPS
GROUP_WIDTH = D_INNER // SSM_GROUPS
LOG2E = math.log2(math.e)
NEG = -1e30
VMEM_LIMIT = 56 * 1024 * 1024

COL_Q, COL_K, COL_V = 0, ATTN_WIDTH, 2 * ATTN_WIDTH
COL_XBC = 3 * ATTN_WIDTH
COL_Z = COL_XBC + CONV_DIM
COL_GATE = COL_Z + D_INNER
PROJ_W = COL_GATE + 2 * D_MODEL

ATTN_TILE = 512
VT_ROWS = 2 * ATTN_HEAD_DIM + 16
MOE_TILE = 512


def _params(*sem):
    return pltpu.CompilerParams(dimension_semantics=sem, vmem_limit_bytes=VMEM_LIMIT)


def _sigmoid(x):
    return 1.0 / (1.0 + jnp.exp(-x))


def _inproj_body(x_ref, g_ref, w_ref, wdt_ref, o_ref, dt_ref, u_scr):
    @pl.when(pl.program_id(1) == 0)
    def _():
        x = x_ref[...]
        u = x * lax.rsqrt(jnp.mean(x * x, axis=-1, keepdims=True) + EPS) * g_ref[...]
        ub = u.astype(BF16)
        u_scr[...] = ub
        dt_ref[...] = jnp.dot(ub, wdt_ref[...], preferred_element_type=F32)

    o_ref[...] = jnp.dot(u_scr[...], w_ref[...], preferred_element_type=F32).astype(o_ref.dtype)


def _in_proj(xf, gain, w_main, w_dt):
    T = xf.shape[0]
    tm = min(1024, T)
    tn = 1024
    return pl.pallas_call(
        _inproj_body,
        grid=(T // tm, PROJ_W // tn),
        in_specs=[
            pl.BlockSpec((tm, D_MODEL), lambda i, j: (i, 0)),
            pl.BlockSpec((1, D_MODEL), lambda i, j: (0, 0)),
            pl.BlockSpec((D_MODEL, tn), lambda i, j: (0, j)),
            pl.BlockSpec((D_MODEL, LANES), lambda i, j: (0, 0)),
        ],
        out_specs=[
            pl.BlockSpec((tm, tn), lambda i, j: (i, j)),
            pl.BlockSpec((tm, LANES), lambda i, j: (i, 0)),
        ],
        out_shape=[
            jax.ShapeDtypeStruct((T, PROJ_W), BF16),
            jax.ShapeDtypeStruct((T, LANES), F32),
        ],
        scratch_shapes=[pltpu.VMEM((tm, D_MODEL), BF16)],
        compiler_params=_params("parallel", "arbitrary"),
        name="in_proj",
    )(xf, gain, w_main, w_dt)


def _split3(x):
    hi = x.astype(BF16).astype(F32)
    r = x - hi
    mid = r.astype(BF16).astype(F32)
    return hi, mid, r - mid


def _attn_prep_body(q_ref, k_ref, v_ref, qg_ref, kg_ref, c_ref, qT_ref, ka_ref, vT_ref, *, tk):
    ts = q_ref.shape[0]
    lane = lax.broadcasted_iota(I32, (ts, LANES), 1)
    lo_half = lane < ATTN_HEAD_DIM

    def half_norm(x, g):
        x2 = x * x
        s_lo = jnp.sum(jnp.where(lo_half, x2, 0.0), axis=-1, keepdims=True)
        s_hi = jnp.sum(jnp.where(lo_half, 0.0, x2), axis=-1, keepdims=True)
        ms = jnp.where(lo_half, s_lo, s_hi) * (1.0 / ATTN_HEAD_DIM)
        return x * lax.rsqrt(ms + QK_EPS) * g

    q = half_norm(q_ref[...].astype(F32), qg_ref[...])
    k = half_norm(k_ref[...].astype(F32), kg_ref[...])

    row = lax.broadcasted_iota(I32, (ts, LANES), 0) + pl.program_id(2) * ts
    bias = c_ref[...] * (row % tk).astype(F32)
    b_hi, b_mid, b_lo = _split3(bias)
    k_tail = jnp.where(lane == ATTN_HEAD_DIM, b_hi,
                       jnp.where(lane == ATTN_HEAD_DIM + 1, b_mid,
                                 jnp.where(lane == ATTN_HEAD_DIM + 2, b_lo, 0.0)))
    q_tail = jnp.where(lane < ATTN_HEAD_DIM + 3, 1.0, 0.0)

    for comp in range(2):
        qc = q if comp == 0 else pltpu.roll(q, ATTN_HEAD_DIM, 1)
        kc = k if comp == 0 else pltpu.roll(k, ATTN_HEAD_DIM, 1)
        qT_ref[comp] = jnp.where(lo_half, qc, q_tail).T.astype(BF16)
        ka_ref[comp] = jnp.where(lo_half, kc, k_tail).astype(BF16)
    dv = 2 * ATTN_HEAD_DIM
    vT_ref[0:dv, :] = v_ref[...].astype(F32).T.astype(BF16)
    vT_ref[dv:VT_ROWS, :] = jnp.ones((VT_ROWS - dv, ts), BF16)


def _attn_prep(proj, qg, kg, c_tab, B, S):
    ts = min(512, S)
    ns = S // ts
    H = ATTN_HEADS
    hb = ATTN_WIDTH // LANES
    return pl.pallas_call(
        functools.partial(_attn_prep_body, tk=ATTN_TILE),
        grid=(B, H, ns),
        in_specs=[
            pl.BlockSpec((ts, LANES), lambda b, h, s: (b * ns + s, h)),
            pl.BlockSpec((ts, LANES), lambda b, h, s: (b * ns + s, hb + h)),
            pl.BlockSpec((ts, LANES), lambda b, h, s: (b * ns + s, 2 * hb + h)),
            pl.BlockSpec((1, LANES), lambda b, h, s: (0, 0)),
            pl.BlockSpec((1, LANES), lambda b, h, s: (0, 0)),
            pl.BlockSpec((None, 1, LANES), lambda b, h, s: (h, 0, 0)),
        ],
        out_specs=[
            pl.BlockSpec((None, None, 2, LANES, ts), lambda b, h, s: (b, h, 0, 0, s)),
            pl.BlockSpec((None, None, 2, ts, LANES), lambda b, h, s: (b, h, 0, s, 0)),
            pl.BlockSpec((None, None, VT_ROWS, ts), lambda b, h, s: (b, h, 0, s)),
        ],
        out_shape=[
            jax.ShapeDtypeStruct((B, H, 2, LANES, S), BF16),
            jax.ShapeDtypeStruct((B, H, 2, S, LANES), BF16),
            jax.ShapeDtypeStruct((B, H, VT_ROWS, S), BF16),
        ],
        compiler_params=_params("parallel", "parallel", "parallel"),
        name="attn_prep",
    )(proj, proj, proj, qg, kg, c_tab)


def _attn_body(lam_ref, qT_ref, ka_ref, vT_ref, c_ref, w_ref, o_ref,
               sa0_ref, sa1_ref, sb0_ref, sb1_ref, acc0_ref, acc1_ref, *, tile, out_scale):
    i = pl.program_id(2)
    accs = (acc0_ref, acc1_ref)
    slot_a, slot_b = (sa0_ref, sa1_ref), (sb0_ref, sb1_ref)
    for acc in accs:
        acc[...] = jnp.zeros_like(acc)
    c_row = c_ref[...]

    def scores(j, slot):
        start = pl.multiple_of(j * tile, tile)
        for comp in range(2):
            slot[comp][...] = jnp.dot(ka_ref[comp, pl.ds(start, tile), :], qT_ref[comp],
                                      preferred_element_type=F32)

    def softmax_pv(j, slot, ms, masked):
        start = pl.multiple_of(j * tile, tile)
        cb = c_row * ((j - i) * tile).astype(F32)
        new_ms = []
        for comp in range(2):
            sT = slot[comp][...]
            if masked:
                key = lax.broadcasted_iota(I32, sT.shape, 0)
                qry = lax.broadcasted_iota(I32, sT.shape, 1)
                sT = jnp.where(key <= qry, sT, NEG)
            m_new = jnp.maximum(ms[comp], jnp.max(sT, axis=0, keepdims=True) + cb)
            alpha = jnp.exp2(ms[comp] - m_new)
            p = jnp.exp2(sT - (m_new - cb)).astype(BF16)
            pv = jnp.dot(vT_ref[:, pl.ds(start, tile)], p, preferred_element_type=F32)
            accs[comp][...] = alpha * accs[comp][...] + pv
            new_ms.append(m_new)
        return tuple(new_ms)

    def finish(slot, ms):
        softmax_pv(i, slot, ms, True)
        dv = 2 * ATTN_HEAD_DIM
        outs = [acc[0:dv, :] / acc[dv:dv + 1, :] for acc in accs]
        o = outs[0] - lam_ref[0] * outs[1]
        msq = jnp.mean(o * o, axis=0, keepdims=True)
        y = o * lax.rsqrt(msq + EPS) * w_ref[...] * out_scale
        o_ref[...] = y.T.astype(o_ref.dtype)

    def body(t, ms):
        j = 2 * t
        scores(j + 1, slot_b)
        ms = softmax_pv(j, slot_a, ms, False)
        scores(j + 2, slot_a)
        return softmax_pv(j + 1, slot_b, ms, False)

    scores(0, slot_a)
    m_init = jnp.full((1, tile), NEG, F32)
    ms = lax.fori_loop(0, i // 2, body, (m_init, m_init))

    @pl.when(i % 2 == 0)
    def _():
        finish(slot_a, ms)

    @pl.when(i % 2 == 1)
    def _():
        scores(i, slot_b)
        finish(slot_b, softmax_pv(i - 1, slot_a, ms, False))


def _attention(lam, qT, ka, vT, c_row_tab, subln_col, B, S, lambda_init):
    tile = ATTN_TILE
    H = ATTN_HEADS
    nq = S // tile
    return pl.pallas_call(
        functools.partial(_attn_body, tile=tile, out_scale=1.0 - lambda_init),
        grid=(B, H, nq),
        in_specs=[
            pl.BlockSpec(memory_space=pltpu.SMEM),
            pl.BlockSpec((None, None, 2, LANES, tile), lambda b, h, i: (b, h, 0, 0, i)),
            pl.BlockSpec((None, None, 2, S, LANES), lambda b, h, i: (b, h, 0, 0, 0)),
            pl.BlockSpec((None, None, VT_ROWS, S), lambda b, h, i: (b, h, 0, 0)),
            pl.BlockSpec((None, 1, tile), lambda b, h, i: (h, 0, 0)),
            pl.BlockSpec((LANES, 1), lambda b, h, i: (0, 0)),
        ],
        out_specs=pl.BlockSpec((tile, LANES), lambda b, h, i: (b * nq + i, h)),
        out_shape=jax.ShapeDtypeStruct((B * S, ATTN_WIDTH), BF16),
        scratch_shapes=[pltpu.VMEM((tile, tile), F32)] * 4 + [pltpu.VMEM((VT_ROWS, tile), F32)] * 2,
        compiler_params=_params("parallel", "parallel", "arbitrary"),
        name="diff_attn",
    )(lam, qT, ka, vT, c_row_tab, subln_col)


def _expand_heads(x, e_ref):
    lane = lax.broadcasted_iota(I32, x.shape, 1)
    hi, mid, lo = _split3(x)
    parts = jnp.where(lane < SSM_HEADS, hi,
                      jnp.where(lane < 2 * SSM_HEADS, mid,
                                jnp.where(lane < 3 * SSM_HEADS, lo, 0.0)))
    return jnp.dot(parts.astype(BF16), e_ref[...], preferred_element_type=F32)


def _ssd_body(xbc_ref, z_ref, dt_ref, cw_ref, cb_ref, dtb_ref, alog_ref, dskip_ref, nw_ref, e_ref,
              o_ref, ext_ref, state_ref, y_ref):
    Q = SSM_CHUNK
    halo = 8

    @pl.when(pl.program_id(1) == 0)
    def _():
        ext_ref[0:halo, :] = jnp.zeros((halo, CONV_DIM), F32)
        state_ref[...] = jnp.zeros_like(state_ref)

    xr = xbc_ref[...].astype(F32)
    ext_ref[halo:halo + Q, :] = xr
    conv = cb_ref[...] + cw_ref[CONV_WIDTH - 1:CONV_WIDTH, :] * xr
    for w in range(CONV_WIDTH - 1):
        off = halo - (CONV_WIDTH - 1) + w
        conv = conv + cw_ref[w:w + 1, :] * ext_ref[off:off + Q, :]
    ext_ref[0:halo, :] = xr[Q - halo:Q, :]
    xc = conv * _sigmoid(conv)
    xs = xc[:, :D_INNER]

    raw = dt_ref[...] + dtb_ref[...]
    dt = jnp.maximum(raw, 0.0) + jnp.log(1.0 + jnp.exp(-jnp.abs(raw)))
    dA = dt * (-jnp.exp(alog_ref[...]))
    t_idx = lax.broadcasted_iota(I32, (Q, Q), 0)
    s_idx = lax.broadcasted_iota(I32, (Q, Q), 1)
    causal = s_idx <= t_idx
    cum = jnp.dot(causal.astype(F32), dA, preferred_element_type=F32,
                  precision=lax.Precision.HIGHEST)
    cumT = cum.T
    ecum = jnp.exp(cum)
    decay = jnp.exp(cum[Q - 1:Q, :] - cum)

    dt_e = _expand_heads(dt, e_ref)
    ecum_e = _expand_heads(ecum, e_ref)
    decay_e = _expand_heads(decay, e_ref)
    xdt = xs * dt_e
    xdt_b = xdt.astype(BF16)
    xdec_b = (xdt * decay_e).astype(BF16)

    lane = lax.broadcasted_iota(I32, (Q, LANES), 1)
    first_head = lane < SSM_HEAD_DIM
    for g in range(SSM_GROUPS):
        bcol = D_INNER + g * SSM_STATE
        ccol = D_INNER + SSM_GROUPS * SSM_STATE + g * SSM_STATE
        Bg = xc[:, bcol:bcol + SSM_STATE].astype(BF16)
        Cg = xc[:, ccol:ccol + SSM_STATE].astype(BF16)
        gs = slice(g * GROUP_WIDTH, (g + 1) * GROUP_WIDTH)
        CB = lax.dot_general(Cg, Bg, (((1,), (1,)), ((), ())), preferred_element_type=F32)
        state = state_ref[g]
        y_off = jnp.dot(Cg, state.astype(BF16), preferred_element_type=F32) * ecum_e[:, gs]
        for pair in range(HEADS_PER_GROUP // 2):
            h0 = g * HEADS_PER_GROUP + 2 * pair
            ms = []
            for h in (h0, h0 + 1):
                diff = cum[:, h:h + 1] - cumT[h:h + 1, :]
                L = jnp.exp(jnp.where(causal, diff, NEG))
                ms.append((CB * L).astype(BF16))
            lhs = jnp.concatenate(ms, axis=1)
            c0 = h0 * SSM_HEAD_DIM
            xp = xdt_b[:, c0:c0 + LANES]
            zero = jnp.zeros_like(xp)
            rhs = jnp.concatenate([jnp.where(first_head, xp, zero),
                                   jnp.where(first_head, zero, xp)], axis=0)
            y_ref[:, c0:c0 + LANES] = (jnp.dot(lhs, rhs, preferred_element_type=F32)
                                       + y_off[:, c0 - g * GROUP_WIDTH:c0 - g * GROUP_WIDTH + LANES])
        upd = lax.dot_general(Bg, xdec_b[:, gs], (((0,), (0,)), ((), ())),
                              preferred_element_type=F32)
        state_ref[g] = state * ecum_e[Q - 1:Q, gs] + upd

    y = y_ref[...] + xs * dskip_ref[...]
    z = z_ref[...].astype(F32)
    y = y * (z * _sigmoid(z))
    for g in range(SSM_GROUPS):
        gs = slice(g * GROUP_WIDTH, (g + 1) * GROUP_WIDTH)
        yg = y[:, gs]
        ms = jnp.mean(yg * yg, axis=-1, keepdims=True)
        o_ref[:, gs] = (yg * lax.rsqrt(ms + EPS) * nw_ref[:, gs]).astype(o_ref.dtype)


def _ssd(proj, dt_raw, conv_w, conv_b, dtb3, alog3, dskip_e, norm_w, expand, B, S):
    Q = SSM_CHUNK
    nc = S // Q
    row = lambda b, c: b * nc + c
    const = lambda b, c: (0, 0)
    return pl.pallas_call(
        _ssd_body,
        grid=(B, nc),
        in_specs=[
            pl.BlockSpec((Q, CONV_DIM), lambda b, c: (row(b, c), COL_XBC // CONV_DIM)),
            pl.BlockSpec((Q, D_INNER), lambda b, c: (row(b, c), COL_Z // D_INNER)),
            pl.BlockSpec((Q, LANES), lambda b, c: (row(b, c), 0)),
            pl.BlockSpec((CONV_WIDTH, CONV_DIM), const),
            pl.BlockSpec((1, CONV_DIM), const),
            pl.BlockSpec((1, LANES), const),
            pl.BlockSpec((1, LANES), const),
            pl.BlockSpec((1, D_INNER), const),
            pl.BlockSpec((1, D_INNER), const),
            pl.BlockSpec((LANES, D_INNER), const),
        ],
        out_specs=pl.BlockSpec((Q, D_INNER), lambda b, c: (row(b, c), 0)),
        out_shape=jax.ShapeDtypeStruct((B * S, D_INNER), BF16),
        scratch_shapes=[
            pltpu.VMEM((Q + 8, CONV_DIM), F32),
            pltpu.VMEM((SSM_GROUPS, SSM_STATE, GROUP_WIDTH), F32),
            pltpu.VMEM((Q, D_INNER), F32),
        ],
        compiler_params=_params("parallel", "arbitrary"),
        name="ssd",
    )(proj, proj, dt_raw, conv_w, conv_b, dtb3, alog3, dskip_e, norm_w, expand)


def _merge_body(attn_ref, ssm_ref, gate_ref, x_ref, wa_ref, ws_ref, wo_ref, n2_ref, wr_ref, br_ref,
                h1_ref, hn_ref, lg_ref):
    gate = gate_ref[...].astype(F32)
    ya = jnp.dot(attn_ref[...], wa_ref[...], preferred_element_type=F32)
    ys = jnp.dot(ssm_ref[...], ws_ref[...], preferred_element_type=F32)
    mixed = _sigmoid(gate[:, :D_MODEL]) * ya + _sigmoid(gate[:, D_MODEL:]) * ys
    h1 = x_ref[...] + jnp.dot(mixed.astype(BF16), wo_ref[...], preferred_element_type=F32)
    h1_ref[...] = h1
    hn = h1 * lax.rsqrt(jnp.mean(h1 * h1, axis=-1, keepdims=True) + EPS) * n2_ref[...]
    hn_ref[...] = hn
    lg_ref[...] = jnp.dot(hn.astype(BF16), wr_ref[...], preferred_element_type=F32) + br_ref[...]


def _merge(attn, ssm, proj, xf, wa, ws, wo, n2, wr, br):
    T = xf.shape[0]
    tm = min(256, T)
    const = lambda i: (0, 0)
    return pl.pallas_call(
        _merge_body,
        grid=(T // tm,),
        in_specs=[
            pl.BlockSpec((tm, ATTN_WIDTH), lambda i: (i, 0)),
            pl.BlockSpec((tm, D_INNER), lambda i: (i, 0)),
            pl.BlockSpec((tm, 2 * D_MODEL), lambda i: (i, COL_GATE // (2 * D_MODEL))),
            pl.BlockSpec((tm, D_MODEL), lambda i: (i, 0)),
            pl.BlockSpec((ATTN_WIDTH, D_MODEL), const),
            pl.BlockSpec((D_INNER, D_MODEL), const),
            pl.BlockSpec((D_MODEL, D_MODEL), const),
            pl.BlockSpec((1, D_MODEL), const),
            pl.BlockSpec((D_MODEL, LANES), const),
            pl.BlockSpec((1, LANES), const),
        ],
        out_specs=[
            pl.BlockSpec((tm, D_MODEL), lambda i: (i, 0)),
            pl.BlockSpec((tm, D_MODEL), lambda i: (i, 0)),
            pl.BlockSpec((tm, LANES), lambda i: (i, 0)),
        ],
        out_shape=[
            jax.ShapeDtypeStruct((T, D_MODEL), F32),
            jax.ShapeDtypeStruct((T, D_MODEL), F32),
            jax.ShapeDtypeStruct((T, LANES), F32),
        ],
        compiler_params=_params("parallel"),
        name="merge",
    )(attn, ssm, proj, xf, wa, ws, wo, n2, wr, br)


def _route_body(lg_ref, idx_ref, w_ref, rank_ref, cnt_ref, base_ref):
    tm = lg_ref.shape[0]

    @pl.when(pl.program_id(0) == 0)
    def _():
        base_ref[...] = jnp.zeros_like(base_ref)

    logit = lg_ref[...].T[:N_EXPERTS, :]
    eio = lax.broadcasted_iota(I32, logit.shape, 0)
    vals, hits = [], []
    for k in range(TOP_K):
        mx = jnp.max(logit, axis=0, keepdims=True)
        idx = jnp.min(jnp.where(logit == mx, eio, N_EXPERTS), axis=0, keepdims=True)
        hit = eio == idx
        logit = jnp.where(hit, -jnp.inf, logit)
        idx_ref[k:k + 1, :] = idx
        vals.append(mx)
        hits.append(hit)
    exps = [jnp.exp(v - vals[0]) for v in vals]
    denom = exps[0] + exps[1] + exps[2] + exps[3]
    for k in range(TOP_K):
        w_ref[k:k + 1, :] = exps[k] / denom

    sel = hits[0] | hits[1] | hits[2] | hits[3]
    before = (lax.broadcasted_iota(I32, (tm, tm), 0) < lax.broadcasted_iota(I32, (tm, tm), 1))
    prefix = jnp.dot(sel.astype(BF16), before.astype(BF16), preferred_element_type=F32)
    base = base_ref[:, 0:1]
    offs = prefix + base
    for k in range(TOP_K):
        rank_ref[k:k + 1, :] = jnp.sum(jnp.where(hits[k], offs, 0.0), axis=0,
                                       keepdims=True).astype(I32)
    new_base = base + jnp.sum(sel.astype(F32), axis=1, keepdims=True)
    base_ref[...] = jnp.broadcast_to(new_base, base_ref.shape)
    cnt_ref[...] = jnp.broadcast_to(new_base, cnt_ref.shape)


def _route(logits):
    T = logits.shape[0]
    tm = min(512, T)
    return pl.pallas_call(
        _route_body,
        grid=(T // tm,),
        in_specs=[pl.BlockSpec((tm, LANES), lambda i: (i, 0))],
        out_specs=[
            pl.BlockSpec((TOP_K, tm), lambda i: (0, i)),
            pl.BlockSpec((TOP_K, tm), lambda i: (0, i)),
            pl.BlockSpec((TOP_K, tm), lambda i: (0, i)),
            pl.BlockSpec((N_EXPERTS, LANES), lambda i: (0, 0)),
        ],
        out_shape=[
            jax.ShapeDtypeStruct((TOP_K, T), I32),
            jax.ShapeDtypeStruct((TOP_K, T), F32),
            jax.ShapeDtypeStruct((TOP_K, T), I32),
            jax.ShapeDtypeStruct((N_EXPERTS, LANES), F32),
        ],
        scratch_shapes=[pltpu.VMEM((N_EXPERTS, LANES), F32)],
        compiler_params=_params("arbitrary"),
        name="route",
    )(logits)


def _row_copy(src_ref, src_row, dst_ref, dst_row, sem):
    return pltpu.make_async_copy(src_ref.at[pl.ds(src_row, 1), :],
                                 dst_ref.at[pl.ds(dst_row, 1), :], sem)


def _dispatch_body(pos_ref, hn_ref, xs_in_ref, xs_ref, sem):
    del xs_in_ref
    tm = hn_ref.shape[0]

    def issue(r, carry):
        for k in range(TOP_K):
            _row_copy(hn_ref, r, xs_ref, pos_ref[r * TOP_K + k], sem).start()
        return carry

    def drain(r, carry):
        for k in range(TOP_K):
            _row_copy(hn_ref, r, xs_ref, pos_ref[r * TOP_K + k], sem).wait()
        return carry

    lax.fori_loop(0, tm, issue, 0)
    lax.fori_loop(0, tm, drain, 0)


def _dispatch(pos_flat, hn, xs_init):
    T = hn.shape[0]
    tm = min(256, T)
    return pl.pallas_call(
        _dispatch_body,
        grid=(T // tm,),
        in_specs=[
            pl.BlockSpec((tm * TOP_K,), lambda i: (i,), memory_space=pltpu.SMEM),
            pl.BlockSpec((tm, D_MODEL), lambda i: (i, 0)),
            pl.BlockSpec(memory_space=pl.ANY),
        ],
        out_specs=pl.BlockSpec(memory_space=pl.ANY),
        out_shape=jax.ShapeDtypeStruct(xs_init.shape, xs_init.dtype),
        scratch_shapes=[pltpu.SemaphoreType.DMA(())],
        input_output_aliases={2: 0},
        compiler_params=_params("arbitrary"),
        name="dispatch",
    )(pos_flat, hn, xs_init)


def _ffn_body(te_ref, nu_ref, xs_ref, wgu_ref, bgu_ref, wd_ref, bd_ref, ys_ref):
    del te_ref

    @pl.when(pl.program_id(0) < nu_ref[0])
    def _():
        gu = jnp.dot(xs_ref[...].astype(BF16), wgu_ref[...], preferred_element_type=F32) + bgu_ref[...]
        glu = jnp.minimum(gu[:, :D_FF], SWIGLU_LIMIT)
        lin = jnp.clip(gu[:, D_FF:], -SWIGLU_LIMIT, SWIGLU_LIMIT)
        act = glu * _sigmoid(SWIGLU_ALPHA * glu) * (lin + 1.0)
        ys_ref[...] = jnp.dot(act.astype(BF16), wd_ref[...], preferred_element_type=F32) + bd_ref[...]


def _ffn(tile_expert, n_used, xs, wgu, bgu, wd, bd):
    M_pad = xs.shape[0]
    tm = MOE_TILE
    n_tiles = M_pad // tm
    row = lambda i, te, nu: (jnp.minimum(i, nu[0] - 1), 0)
    return pl.pallas_call(
        _ffn_body,
        grid_spec=pltpu.PrefetchScalarGridSpec(
            num_scalar_prefetch=2,
            grid=(n_tiles,),
            in_specs=[
                pl.BlockSpec((tm, D_MODEL), row),
                pl.BlockSpec((None, D_MODEL, 2 * D_FF), lambda i, te, nu: (te[i], 0, 0)),
                pl.BlockSpec((None, 1, 2 * D_FF), lambda i, te, nu: (te[i], 0, 0)),
                pl.BlockSpec((None, D_FF, D_MODEL), lambda i, te, nu: (te[i], 0, 0)),
                pl.BlockSpec((None, 1, D_MODEL), lambda i, te, nu: (te[i], 0, 0)),
            ],
            out_specs=pl.BlockSpec((tm, D_MODEL), row),
        ),
        out_shape=jax.ShapeDtypeStruct((M_pad, D_MODEL), F32),
        compiler_params=_params("arbitrary"),
        name="expert_ffn",
    )(tile_expert, n_used, xs, wgu, bgu, wd, bd)


def _combine_body(pos_ref, w_ref, h1_ref, ys_ref, o_ref, buf_ref, sem):
    tm = h1_ref.shape[0]

    def issue(r, carry):
        for k in range(TOP_K):
            _row_copy(ys_ref, pos_ref[r * TOP_K + k], buf_ref.at[k], r, sem).start()
        return carry

    def drain(r, carry):
        for k in range(TOP_K):
            _row_copy(ys_ref, pos_ref[r * TOP_K + k], buf_ref.at[k], r, sem).wait()
        return carry

    lax.fori_loop(0, tm, issue, 0)
    lax.fori_loop(0, tm, drain, 0)
    w = w_ref[...]
    out = h1_ref[...]
    for k in range(TOP_K):
        out = out + w[:, k:k + 1] * buf_ref[k]
    o_ref[...] = out


def _combine(pos_flat, w_tok, h1, ys):
    T = h1.shape[0]
    tm = min(256, T)
    return pl.pallas_call(
        _combine_body,
        grid=(T // tm,),
        in_specs=[
            pl.BlockSpec((tm * TOP_K,), lambda i: (i,), memory_space=pltpu.SMEM),
            pl.BlockSpec((tm, TOP_K), lambda i: (i, 0)),
            pl.BlockSpec((tm, D_MODEL), lambda i: (i, 0)),
            pl.BlockSpec(memory_space=pl.ANY),
        ],
        out_specs=pl.BlockSpec((tm, D_MODEL), lambda i: (i, 0)),
        out_shape=jax.ShapeDtypeStruct((T, D_MODEL), F32),
        scratch_shapes=[pltpu.VMEM((TOP_K, tm, D_MODEL), F32), pltpu.SemaphoreType.DMA(())],
        compiler_params=_params("arbitrary"),
        name="combine",
    )(pos_flat, w_tok, h1, ys)


def _layer(h, l, p):
    B, S, _ = h.shape
    T = B * S
    lambda_init = 0.8 - 0.6 * math.exp(-0.3 * l)
    hf = h.reshape(T, D_MODEL)

    w_in = p["w_in"]
    sizes = [ATTN_WIDTH, ATTN_WIDTH, ATTN_WIDTH, D_INNER, CONV_DIM, SSM_HEADS, 2 * D_MODEL]
    offs = [0]
    for s in sizes:
        offs.append(offs[-1] + s)
    wq, wk, wv, wz, wxbc, wdt, wg = (w_in[:, offs[n]:offs[n + 1]] for n in range(7))
    w_main = jnp.concatenate([wq, wk, wv, wxbc, wz, wg], axis=1).astype(BF16)
    pad = jnp.zeros((D_MODEL, LANES - 3 * SSM_HEADS), F32)
    w_dt = jnp.concatenate([wdt, wdt, wdt, pad], axis=1).astype(BF16)

    proj, dt_raw = _in_proj(hf, p["norm1_w"].reshape(1, D_MODEL), w_main, w_dt)

    qk_scale = ATTN_HEAD_DIM ** -0.5 * LOG2E
    qg = (jnp.tile(p["q_norm_w"], 2) * qk_scale).reshape(1, LANES)
    kg = jnp.tile(p["k_norm_w"], 2).reshape(1, LANES)
    slopes = jnp.power(2.0, -8.0 * (jnp.arange(ATTN_HEADS, dtype=F32) + 1.0) / ATTN_HEADS) * LOG2E
    c_tab = jnp.broadcast_to(slopes[:, None, None], (ATTN_HEADS, 1, LANES))
    c_row_tab = jnp.broadcast_to(slopes[:, None, None], (ATTN_HEADS, 1, ATTN_TILE))
    lam = (jnp.exp(jnp.sum(p["lambda_q1"] * p["lambda_k1"]))
           - jnp.exp(jnp.sum(p["lambda_q2"] * p["lambda_k2"])) + lambda_init).reshape(1)
    qT, ka, vT = _attn_prep(proj, qg, kg, c_tab, B, S)
    attn = _attention(lam, qT, ka, vT, c_row_tab, p["subln_w"].reshape(LANES, 1), B, S, lambda_init)

    rep3 = lambda v: jnp.concatenate([v, v, v, jnp.zeros((LANES - 3 * SSM_HEADS,), F32)]).reshape(1, LANES)
    head_of_row = jnp.arange(LANES) % SSM_HEADS
    head_of_col = jnp.arange(D_INNER) // SSM_HEAD_DIM
    expand = ((head_of_row[:, None] == head_of_col[None, :])
              & (jnp.arange(LANES)[:, None] < 3 * SSM_HEADS)).astype(BF16)
    ssm = _ssd(proj, dt_raw, p["conv_w"], p["conv_b"].reshape(1, CONV_DIM), rep3(p["dt_bias"]),
               rep3(p["a_log"]), jnp.repeat(p["d_skip"], SSM_HEAD_DIM).reshape(1, D_INNER),
               p["ssm_norm_w"].reshape(1, D_INNER), expand, B, S)

    wr = jnp.pad(p["w_router"], ((0, 0), (0, LANES - N_EXPERTS))).astype(BF16)
    br = jnp.pad(p["b_router"], (0, LANES - N_EXPERTS)).reshape(1, LANES)
    h1, hn, logits = _merge(attn, ssm, proj, hf, p["w_attn_proj"].astype(BF16),
                            p["w_ssm_proj"].astype(BF16), p["w_out"].astype(BF16),
                            p["norm2_w"].reshape(1, D_MODEL), wr, br)

    idx, w_top, rank, cnt = _route(logits)
    counts = cnt[:, 0].astype(I32)
    tiles_per = (counts + MOE_TILE - 1) // MOE_TILE
    tile_end = jnp.cumsum(tiles_per)
    group_start = (tile_end - tiles_per) * MOE_TILE
    M_pad = T * TOP_K + N_EXPERTS * MOE_TILE
    n_tiles = M_pad // MOE_TILE
    tile_expert = jnp.minimum(jnp.sum(jnp.arange(n_tiles)[:, None] >= tile_end[None, :], axis=1),
                              N_EXPERTS - 1).astype(I32)
    n_used = tile_end[-1:].astype(I32)
    experts = jnp.arange(N_EXPERTS, dtype=I32)
    pos = rank + jnp.sum(jnp.where(idx[..., None] == experts, group_start, 0), axis=-1)
    pos_flat = pos.T.reshape(T * TOP_K)

    xs = _dispatch(pos_flat, hn, jnp.zeros((M_pad, D_MODEL), F32))
    ys = _ffn(tile_expert, n_used, xs, p["w_gate_up"].astype(BF16),
              p["b_gate_up"].reshape(N_EXPERTS, 1, 2 * D_FF), p["w_down"].astype(BF16),
              p["b_down"].reshape(N_EXPERTS, 1, D_MODEL))
    out = _combine(pos_flat, w_top.T, h1, ys)
    return out.reshape(B, S, D_MODEL)


def kernel(x, norm1_w, w_in, q_norm_w, k_norm_w, lambda_q1, lambda_k1, lambda_q2, lambda_k2, subln_w, conv_w, conv_b, dt_bias, a_log, d_skip, ssm_norm_w, w_attn_proj, w_ssm_proj, w_out, norm2_w, w_router, b_router, w_gate_up, b_gate_up, w_down, b_down):
    params = dict(norm1_w=norm1_w, w_in=w_in, q_norm_w=q_norm_w, k_norm_w=k_norm_w,
                  lambda_q1=lambda_q1, lambda_k1=lambda_k1, lambda_q2=lambda_q2, lambda_k2=lambda_k2,
                  subln_w=subln_w, conv_w=conv_w, conv_b=conv_b, dt_bias=dt_bias, a_log=a_log,
                  d_skip=d_skip, ssm_norm_w=ssm_norm_w, w_attn_proj=w_attn_proj,
                  w_ssm_proj=w_ssm_proj, w_out=w_out, norm2_w=norm2_w, w_router=w_router,
                  b_router=b_router, w_gate_up=w_gate_up, b_gate_up=b_gate_up, w_down=w_down,
                  b_down=b_down)
    h = x
    for l in range(w_in.shape[0]):
        h = _layer(h, l, {k: v[l] for k, v in params.items()})
    return h
```

```python
import functools
import math

import jax
import jax.numpy as jnp
from jax import lax
from jax.experimental import pallas as pl
from jax.experimental.pallas import tpu as pltpu

F32, BF16, I32 = jnp.float32, jnp.bfloat16, jnp.int32

D_MODEL = 1024
ATTN_HEADS = 8
ATTN_HEAD_DIM = 64
ATTN_WIDTH = ATTN_HEADS * 2 * ATTN_HEAD_DIM
SSM_EXPAND = 2
D_INNER = SSM_EXPAND * D_MODEL
SSM_HEAD_DIM = 64
SSM_HEADS = D_INNER // SSM_HEAD_DIM
SSM_GROUPS = 4
SSM_STATE = 128
CONV_WIDTH = 4
CONV_DIM = D_INNER + 2 * SSM_GROUPS * SSM_STATE
SSM_CHUNK = 128
N_EXPERTS = 32
TOP_K = 4
D_FF = D_MODEL
SWIGLU_LIMIT = 7.0
SWIGLU_ALPHA = 1.702
EPS = 1e-5
QK_EPS = 1e-6

LANES = 128
HEADS_PER_GROUP = SSM_HEADS // SSM_GROUPS
GROUP_WIDTH = D_INNER // SSM_GROUPS
LOG2E = math.log2(math.e)
NEG = -1e30
VMEM_LIMIT = 56 * 1024 * 1024

COL_Q, COL_K, COL_V = 0, ATTN_WIDTH, 2 * ATTN_WIDTH
COL_XBC = 3 * ATTN_WIDTH
COL_Z = COL_XBC + CONV_DIM
COL_GATE = COL_Z + D_INNER
PROJ_W = COL_GATE + 2 * D_MODEL

ATTN_TILE = 512
VT_ROWS = 2 * ATTN_HEAD_DIM + 16
MOE_TILE = 512
ROUTE_TILE = 256
ROW_ALIGN = 8
ROW_CHUNKS = (8, 16, 32, 64, 128, 256)


def _sort_rows(tm):
    return TOP_K * tm + N_EXPERTS * ROW_ALIGN


def _params(*sem):
    return pltpu.CompilerParams(dimension_semantics=sem, vmem_limit_bytes=VMEM_LIMIT)


def _sigmoid(x):
    return 1.0 / (1.0 + jnp.exp(-x))


def _inproj_body(x_ref, g_ref, w_ref, wdt_ref, o_ref, dt_ref, u_scr):
    @pl.when(pl.program_id(1) == 0)
    def _():
        x = x_ref[...]
        u = x * lax.rsqrt(jnp.mean(x * x, axis=-1, keepdims=True) + EPS) * g_ref[...]
        ub = u.astype(BF16)
        u_scr[...] = ub
        dt_ref[...] = jnp.dot(ub, wdt_ref[...], preferred_element_type=F32)

    o_ref[...] = jnp.dot(u_scr[...], w_ref[...], preferred_element_type=F32).astype(o_ref.dtype)


def _in_proj(xf, gain, w_main, w_dt):
    T = xf.shape[0]
    tm = min(1024, T)
    tn = 1024
    return pl.pallas_call(
        _inproj_body,
        grid=(T // tm, PROJ_W // tn),
        in_specs=[
            pl.BlockSpec((tm, D_MODEL), lambda i, j: (i, 0)),
            pl.BlockSpec((1, D_MODEL), lambda i, j: (0, 0)),
            pl.BlockSpec((D_MODEL, tn), lambda i, j: (0, j)),
            pl.BlockSpec((D_MODEL, LANES), lambda i, j: (0, 0)),
        ],
        out_specs=[
            pl.BlockSpec((tm, tn), lambda i, j: (i, j)),
            pl.BlockSpec((tm, LANES), lambda i, j: (i, 0)),
        ],
        out_shape=[
            jax.ShapeDtypeStruct((T, PROJ_W), BF16),
            jax.ShapeDtypeStruct((T, LANES), F32),
        ],
        scratch_shapes=[pltpu.VMEM((tm, D_MODEL), BF16)],
        compiler_params=_params("parallel", "arbitrary"),
        name="in_proj",
    )(xf, gain, w_main, w_dt)


def _split3(x):
    hi = x.astype(BF16).astype(F32)
    r = x - hi
    mid = r.astype(BF16).astype(F32)
    return hi, mid, r - mid


def _attn_prep_body(q_ref, k_ref, v_ref, qg_ref, kg_ref, c_ref, qT_ref, ka_ref, vT_ref, *, tk):
    ts = q_ref.shape[0]
    lane = lax.broadcasted_iota(I32, (ts, LANES), 1)
    lo_half = lane < ATTN_HEAD_DIM

    def half_norm(x, g):
        x2 = x * x
        s_lo = jnp.sum(jnp.where(lo_half, x2, 0.0), axis=-1, keepdims=True)
        s_hi = jnp.sum(jnp.where(lo_half, 0.0, x2), axis=-1, keepdims=True)
        ms = jnp.where(lo_half, s_lo, s_hi) * (1.0 / ATTN_HEAD_DIM)
        return x * lax.rsqrt(ms + QK_EPS) * g

    q = half_norm(q_ref[...].astype(F32), qg_ref[...])
    k = half_norm(k_ref[...].astype(F32), kg_ref[...])

    row = lax.broadcasted_iota(I32, (ts, LANES), 0) + pl.program_id(2) * ts
    bias = c_ref[...] * (row % tk).astype(F32)
    b_hi, b_mid, b_lo = _split3(bias)
    k_tail = jnp.where(lane == ATTN_HEAD_DIM, b_hi,
                       jnp.where(lane == ATTN_HEAD_DIM + 1, b_mid,
                                 jnp.where(lane == ATTN_HEAD_DIM + 2, b_lo, 0.0)))
    q_tail = jnp.where(lane < ATTN_HEAD_DIM + 3, 1.0, 0.0)

    for comp in range(2):
        qc = q if comp == 0 else pltpu.roll(q, ATTN_HEAD_DIM, 1)
        kc = k if comp == 0 else pltpu.roll(k, ATTN_HEAD_DIM, 1)
        qT_ref[comp] = jnp.where(lo_half, qc, q_tail).T.astype(BF16)
        ka_ref[comp] = jnp.where(lo_half, kc, k_tail).astype(BF16)
    dv = 2 * ATTN_HEAD_DIM
    vT_ref[0:dv, :] = v_ref[...].astype(F32).T.astype(BF16)
    vT_ref[dv:VT_ROWS, :] = jnp.ones((VT_ROWS - dv, ts), BF16)


def _attn_prep(proj, qg, kg, c_tab, B, S):
    ts = min(512, S)
    ns = S // ts
    H = ATTN_HEADS
    hb = ATTN_WIDTH // LANES
    return pl.pallas_call(
        functools.partial(_attn_prep_body, tk=ATTN_TILE),
        grid=(B, H, ns),
        in_specs=[
            pl.BlockSpec((ts, LANES), lambda b, h, s: (b * ns + s, h)),
            pl.BlockSpec((ts, LANES), lambda b, h, s: (b * ns + s, hb + h)),
            pl.BlockSpec((ts, LANES), lambda b, h, s: (b * ns + s, 2 * hb + h)),
            pl.BlockSpec((1, LANES), lambda b, h, s: (0, 0)),
            pl.BlockSpec((1, LANES), lambda b, h, s: (0, 0)),
            pl.BlockSpec((None, 1, LANES), lambda b, h, s: (h, 0, 0)),
        ],
        out_specs=[
            pl.BlockSpec((None, None, 2, LANES, ts), lambda b, h, s: (b, h, 0, 0, s)),
            pl.BlockSpec((None, None, 2, ts, LANES), lambda b, h, s: (b, h, 0, s, 0)),
            pl.BlockSpec((None, None, VT_ROWS, ts), lambda b, h, s: (b, h, 0, s)),
        ],
        out_shape=[
            jax.ShapeDtypeStruct((B, H, 2, LANES, S), BF16),
            jax.ShapeDtypeStruct((B, H, 2, S, LANES), BF16),
            jax.ShapeDtypeStruct((B, H, VT_ROWS, S), BF16),
        ],
        compiler_params=_params("parallel", "parallel", "parallel"),
        name="attn_prep",
    )(proj, proj, proj, qg, kg, c_tab)


def _attn_body(lam_ref, qT_ref, ka_ref, vT_ref, c_ref, w_ref, o_ref,
               sa0_ref, sa1_ref, sb0_ref, sb1_ref, acc0_ref, acc1_ref, *, tile, out_scale):
    i = pl.program_id(2)
    accs = (acc0_ref, acc1_ref)
    slot_a, slot_b = (sa0_ref, sa1_ref), (sb0_ref, sb1_ref)
    for acc in accs:
        acc[...] = jnp.zeros_like(acc)
    c_row = c_ref[...]

    def scores(j, slot):
        start = pl.multiple_of(j * tile, tile)
        for comp in range(2):
            slot[comp][...] = jnp.dot(ka_ref[comp, pl.ds(start, tile), :], qT_ref[comp],
                                      preferred_element_type=F32)

    def softmax_pv(j, slot, ms, masked):
        start = pl.multiple_of(j * tile, tile)
        cb = c_row * ((j - i) * tile).astype(F32)
        new_ms = []
        for comp in range(2):
            sT = slot[comp][...]
            if masked:
                key = lax.broadcasted_iota(I32, sT.shape, 0)
                qry = lax.broadcasted_iota(I32, sT.shape, 1)
                sT = jnp.where(key <= qry, sT, NEG)
            m_new = jnp.maximum(ms[comp], jnp.max(sT, axis=0, keepdims=True) + cb)
            alpha = jnp.exp2(ms[comp] - m_new)
            p = jnp.exp2(sT - (m_new - cb)).astype(BF16)
            pv = jnp.dot(vT_ref[:, pl.ds(start, tile)], p, preferred_element_type=F32)
            accs[comp][...] = alpha * accs[comp][...] + pv
            new_ms.append(m_new)
        return tuple(new_ms)

    def finish(slot, ms):
        softmax_pv(i, slot, ms, True)
        dv = 2 * ATTN_HEAD_DIM
        outs = [acc[0:dv, :] / acc[dv:dv + 1, :] for acc in accs]
        o = outs[0] - lam_ref[0] * outs[1]
        msq = jnp.mean(o * o, axis=0, keepdims=True)
        y = o * lax.rsqrt(msq + EPS) * w_ref[...] * out_scale
        o_ref[...] = y.T.astype(o_ref.dtype)

    def body(t, ms):
        j = 2 * t
        scores(j + 1, slot_b)
        ms = softmax_pv(j, slot_a, ms, False)
        scores(j + 2, slot_a)
        return softmax_pv(j + 1, slot_b, ms, False)

    scores(0, slot_a)
    m_init = jnp.full((1, tile), NEG, F32)
    ms = lax.fori_loop(0, i // 2, body, (m_init, m_init))

    @pl.when(i % 2 == 0)
    def _():
        finish(slot_a, ms)

    @pl.when(i % 2 == 1)
    def _():
        scores(i, slot_b)
        finish(slot_b, softmax_pv(i - 1, slot_a, ms, False))


def _attention(lam, qT, ka, vT, c_row_tab, subln_col, B, S, lambda_init):
    tile = ATTN_TILE
    H = ATTN_HEADS
    nq = S // tile
    return pl.pallas_call(
        functools.partial(_attn_body, tile=tile, out_scale=1.0 - lambda_init),
        grid=(B, H, nq),
        in_specs=[
            pl.BlockSpec(memory_space=pltpu.SMEM),
            pl.BlockSpec((None, None, 2, LANES, tile), lambda b, h, i: (b, h, 0, 0, i)),
            pl.BlockSpec((None, None, 2, S, LANES), lambda b, h, i: (b, h, 0, 0, 0)),
            pl.BlockSpec((None, None, VT_ROWS, S), lambda b, h, i: (b, h, 0, 0)),
            pl.BlockSpec((None, 1, tile), lambda b, h, i: (h, 0, 0)),
            pl.BlockSpec((LANES, 1), lambda b, h, i: (0, 0)),
        ],
        out_specs=pl.BlockSpec((tile, LANES), lambda b, h, i: (b * nq + i, h)),
        out_shape=jax.ShapeDtypeStruct((B * S, ATTN_WIDTH), BF16),
        scratch_shapes=[pltpu.VMEM((tile, tile), F32)] * 4 + [pltpu.VMEM((VT_ROWS, tile), F32)] * 2,
        compiler_params=_params("parallel", "parallel", "arbitrary"),
        name="diff_attn",
    )(lam, qT, ka, vT, c_row_tab, subln_col)


def _expand_heads(x, e_ref):
    lane = lax.broadcasted_iota(I32, x.shape, 1)
    hi, mid, lo = _split3(x)
    parts = jnp.where(lane < SSM_HEADS, hi,
                      jnp.where(lane < 2 * SSM_HEADS, mid,
                                jnp.where(lane < 3 * SSM_HEADS, lo, 0.0)))
    return jnp.dot(parts.astype(BF16), e_ref[...], preferred_element_type=F32)


def _ssd_body(xbc_ref, z_ref, dt_ref, cw_ref, cb_ref, dtb_ref, alog_ref, dskip_ref, nw_ref, e_ref,
              o_ref, ext_ref, state_ref, y_ref):
    Q = SSM_CHUNK
    halo = 8

    @pl.when(pl.program_id(1) == 0)
    def _():
        ext_ref[0:halo, :] = jnp.zeros((halo, CONV_DIM), F32)
        state_ref[...] = jnp.zeros_like(state_ref)

    xr = xbc_ref[...].astype(F32)
    ext_ref[halo:halo + Q, :] = xr
    conv = cb_ref[...] + cw_ref[CONV_WIDTH - 1:CONV_WIDTH, :] * xr
    for w in range(CONV_WIDTH - 1):
        off = halo - (CONV_WIDTH - 1) + w
        conv = conv + cw_ref[w:w + 1, :] * ext_ref[off:off + Q, :]
    ext_ref[0:halo, :] = xr[Q - halo:Q, :]
    xc = conv * _sigmoid(conv)
    xs = xc[:, :D_INNER]

    raw = dt_ref[...] + dtb_ref[...]
    dt = jnp.maximum(raw, 0.0) + jnp.log(1.0 + jnp.exp(-jnp.abs(raw)))
    dA = dt * (-jnp.exp(alog_ref[...]))
    t_idx = lax.broadcasted_iota(I32, (Q, Q), 0)
    s_idx = lax.broadcasted_iota(I32, (Q, Q), 1)
    causal = s_idx <= t_idx
    cum = jnp.dot(causal.astype(F32), dA, preferred_element_type=F32,
                  precision=lax.Precision.HIGHEST)
    cumT = cum.T
    ecum = jnp.exp(cum)
    decay = jnp.exp(cum[Q - 1:Q, :] - cum)

    dt_e = _expand_heads(dt, e_ref)
    ecum_e = _expand_heads(ecum, e_ref)
    decay_e = _expand_heads(decay, e_ref)
    xdt = xs * dt_e
    xdt_b = xdt.astype(BF16)
    xdec_b = (xdt * decay_e).astype(BF16)

    lane = lax.broadcasted_iota(I32, (Q, LANES), 1)
    first_head = lane < SSM_HEAD_DIM
    for g in range(SSM_GROUPS):
        bcol = D_INNER + g * SSM_STATE
        ccol = D_INNER + SSM_GROUPS * SSM_STATE + g * SSM_STATE
        Bg = xc[:, bcol:bcol + SSM_STATE].astype(BF16)
        Cg = xc[:, ccol:ccol + SSM_STATE].astype(BF16)
        gs = slice(g * GROUP_WIDTH, (g + 1) * GROUP_WIDTH)
        CB = lax.dot_general(Cg, Bg, (((1,), (1,)), ((), ())), preferred_element_type=F32)
        state = state_ref[g]
        y_off = jnp.dot(Cg, state.astype(BF16), preferred_element_type=F32) * ecum_e[:, gs]
        for pair in range(HEADS_PER_GROUP // 2):
            h0 = g * HEADS_PER_GROUP + 2 * pair
            ms = []
            for h in (h0, h0 + 1):
                diff = cum[:, h:h + 1] - cumT[h:h + 1, :]
                L = jnp.exp(jnp.where(causal, diff, NEG))
                ms.append((CB * L).astype(BF16))
            lhs = jnp.concatenate(ms, axis=1)
            c0 = h0 * SSM_HEAD_DIM
            xp = xdt_b[:, c0:c0 + LANES]
            zero = jnp.zeros_like(xp)
            rhs = jnp.concatenate([jnp.where(first_head, xp, zero),
                                   jnp.where(first_head, zero, xp)], axis=0)
            y_ref[:, c0:c0 + LANES] = (jnp.dot(lhs, rhs, preferred_element_type=F32)
                                       + y_off[:, c0 - g * GROUP_WIDTH:c0 - g * GROUP_WIDTH + LANES])
        upd = lax.dot_general(Bg, xdec_b[:, gs], (((0,), (0,)), ((), ())),
                              preferred_element_type=F32)
        state_ref[g] = state * ecum_e[Q - 1:Q, gs] + upd

    y = y_ref[...] + xs * dskip_ref[...]
    z = z_ref[...].astype(F32)
    y = y * (z * _sigmoid(z))
    for g in range(SSM_GROUPS):
        gs = slice(g * GROUP_WIDTH, (g + 1) * GROUP_WIDTH)
        yg = y[:, gs]
        ms = jnp.mean(yg * yg, axis=-1, keepdims=True)
        o_ref[:, gs] = (yg * lax.rsqrt(ms + EPS) * nw_ref[:, gs]).astype(o_ref.dtype)


def _ssd(proj, dt_raw, conv_w, conv_b, dtb3, alog3, dskip_e, norm_w, expand, B, S):
    Q = SSM_CHUNK
    nc = S // Q
    row = lambda b, c: b * nc + c
    const = lambda b, c: (0, 0)
    return pl.pallas_call(
        _ssd_body,
        grid=(B, nc),
        in_specs=[
            pl.BlockSpec((Q, CONV_DIM), lambda b, c: (row(b, c), COL_XBC // CONV_DIM)),
            pl.BlockSpec((Q, D_INNER), lambda b, c: (row(b, c), COL_Z // D_INNER)),
            pl.BlockSpec((Q, LANES), lambda b, c: (row(b, c), 0)),
            pl.BlockSpec((CONV_WIDTH, CONV_DIM), const),
            pl.BlockSpec((1, CONV_DIM), const),
            pl.BlockSpec((1, LANES), const),
            pl.BlockSpec((1, LANES), const),
            pl.BlockSpec((1, D_INNER), const),
            pl.BlockSpec((1, D_INNER), const),
            pl.BlockSpec((LANES, D_INNER), const),
        ],
        out_specs=pl.BlockSpec((Q, D_INNER), lambda b, c: (row(b, c), 0)),
        out_shape=jax.ShapeDtypeStruct((B * S, D_INNER), BF16),
        scratch_shapes=[
            pltpu.VMEM((Q + 8, CONV_DIM), F32),
            pltpu.VMEM((SSM_GROUPS, SSM_STATE, GROUP_WIDTH), F32),
            pltpu.VMEM((Q, D_INNER), F32),
        ],
        compiler_params=_params("parallel", "arbitrary"),
        name="ssd",
    )(proj, proj, dt_raw, conv_w, conv_b, dtb3, alog3, dskip_e, norm_w, expand)


def _merge_body(attn_ref, ssm_ref, gate_ref, x_ref, wa_ref, ws_ref, wo_ref, n2_ref, wr_ref, br_ref,
                h1_ref, hn_ref, lg_ref):
    gate = gate_ref[...].astype(F32)
    ya = jnp.dot(attn_ref[...], wa_ref[...], preferred_element_type=F32)
    ys = jnp.dot(ssm_ref[...], ws_ref[...], preferred_element_type=F32)
    mixed = _sigmoid(gate[:, :D_MODEL]) * ya + _sigmoid(gate[:, D_MODEL:]) * ys
    h1 = x_ref[...] + jnp.dot(mixed.astype(BF16), wo_ref[...], preferred_element_type=F32)
    h1_ref[...] = h1
    hn = h1 * lax.rsqrt(jnp.mean(h1 * h1, axis=-1, keepdims=True) + EPS) * n2_ref[...]
    hn_ref[...] = hn
    lg_ref[...] = jnp.dot(hn.astype(BF16), wr_ref[...], preferred_element_type=F32) + br_ref[...]


def _merge(attn, ssm, proj, xf, wa, ws, wo, n2, wr, br):
    T = xf.shape[0]
    tm = min(256, T)
    const = lambda i: (0, 0)
    return pl.pallas_call(
        _merge_body,
        grid=(T // tm,),
        in_specs=[
            pl.BlockSpec((tm, ATTN_WIDTH), lambda i: (i, 0)),
            pl.BlockSpec((tm, D_INNER), lambda i: (i, 0)),
            pl.BlockSpec((tm, 2 * D_MODEL), lambda i: (i, COL_GATE // (2 * D_MODEL))),
            pl.BlockSpec((tm, D_MODEL), lambda i: (i, 0)),
            pl.BlockSpec((ATTN_WIDTH, D_MODEL), const),
            pl.BlockSpec((D_INNER, D_MODEL), const),
            pl.BlockSpec((D_MODEL, D_MODEL), const),
            pl.BlockSpec((1, D_MODEL), const),
            pl.BlockSpec((D_MODEL, LANES), const),
            pl.BlockSpec((1, LANES), const),
        ],
        out_specs=[
            pl.BlockSpec((tm, D_MODEL), lambda i: (i, 0)),
            pl.BlockSpec((tm, D_MODEL), lambda i: (i, 0)),
            pl.BlockSpec((tm, LANES), lambda i: (i, 0)),
        ],
        out_shape=[
            jax.ShapeDtypeStruct((T, D_MODEL), F32),
            jax.ShapeDtypeStruct((T, D_MODEL), F32),
            jax.ShapeDtypeStruct((T, LANES), F32),
        ],
        compiler_params=_params("parallel"),
        name="merge",
    )(attn, ssm, proj, xf, wa, ws, wo, n2, wr, br)


def _route_body(lg_ref, idx_ref, w_ref, lpos_ref):
    tm = lg_ref.shape[0]
    logit = lg_ref[...].T[:N_EXPERTS, :]
    eio = lax.broadcasted_iota(I32, logit.shape, 0)
    vals, hits = [], []
    for k in range(TOP_K):
        mx = jnp.max(logit, axis=0, keepdims=True)
        idx = jnp.min(jnp.where(logit == mx, eio, N_EXPERTS), axis=0, keepdims=True)
        hit = eio == idx
        logit = jnp.where(hit, -jnp.inf, logit)
        idx_ref[k:k + 1, :] = idx
        vals.append(mx)
        hits.append(hit)
    exps = [jnp.exp(v - vals[0]) for v in vals]
    denom = exps[0] + exps[1] + exps[2] + exps[3]
    for k in range(TOP_K):
        w_ref[k:k + 1, :] = exps[k] / denom

    sel = (hits[0] | hits[1] | hits[2] | hits[3]).astype(BF16)
    before = (lax.broadcasted_iota(I32, (tm, tm), 0) < lax.broadcasted_iota(I32, (tm, tm), 1))
    prefix = jnp.dot(sel, before.astype(BF16), preferred_element_type=F32)
    count = jnp.sum(sel.astype(F32), axis=1, keepdims=True)
    count = jnp.floor((count + (ROW_ALIGN - 1)) * (1.0 / ROW_ALIGN)) * ROW_ALIGN
    count = jnp.broadcast_to(count, (N_EXPERTS, LANES))
    lower = (lax.broadcasted_iota(I32, (N_EXPERTS, N_EXPERTS), 1)
             < lax.broadcasted_iota(I32, (N_EXPERTS, N_EXPERTS), 0))
    start = jnp.dot(lower.astype(BF16), count.astype(BF16), preferred_element_type=F32)[:, 0:1]
    offs = prefix + start
    for k in range(TOP_K):
        lpos_ref[k:k + 1, :] = jnp.sum(jnp.where(hits[k], offs, 0.0), axis=0,
                                       keepdims=True).astype(I32)


def _route(logits):
    T = logits.shape[0]
    tm = min(ROUTE_TILE, T)
    spec = pl.BlockSpec((TOP_K, tm), lambda i: (0, i))
    return pl.pallas_call(
        _route_body,
        grid=(T // tm,),
        in_specs=[pl.BlockSpec((tm, LANES), lambda i: (i, 0))],
        out_specs=[spec, spec, spec],
        out_shape=[
            jax.ShapeDtypeStruct((TOP_K, T), I32),
            jax.ShapeDtypeStruct((TOP_K, T), F32),
            jax.ShapeDtypeStruct((TOP_K, T), I32),
        ],
        compiler_params=_params("parallel"),
        name="route",
    )(logits)


def _for_row_chunks(count, fn):
    for size in ROW_CHUNKS:
        @pl.when((count & size) != 0)
        def _():
            fn(count & (size - 1), size)


def _rows_copy(src_ref, src_row, dst_ref, dst_row, size, sem):
    return pltpu.make_async_copy(src_ref.at[pl.ds(pl.multiple_of(src_row, ROW_ALIGN), size), :],
                                 dst_ref.at[pl.ds(pl.multiple_of(dst_row, ROW_ALIGN), size), :], sem)


def _dispatch_body(row_ref, cnt_ref, pad_row_ref, pad_cnt_ref, lpos_ref, hn_ref, xs_ref,
                   sorted_ref, zero_ref, sems, pad_sem):
    t = pl.program_id(0)
    nt = pl.num_programs(0)
    tm = hn_ref.shape[0]
    slot = t % 2

    @pl.when(t == 0)
    def _():
        zero_ref[...] = jnp.zeros_like(zero_ref)
        for wait in (False, True):
            def pad(e, carry):
                def one(off, size):
                    cp = _rows_copy(zero_ref, 0, xs_ref, pad_row_ref[e] + off, size, pad_sem)
                    cp.wait() if wait else cp.start()
                _for_row_chunks(pad_cnt_ref[e], one)
                return carry
            lax.fori_loop(0, N_EXPERTS, pad, 0)

    lp = lpos_ref[...]
    r = lax.broadcasted_iota(I32, (_sort_rows(tm), tm), 0)
    onehot = jnp.where(r == lp[0:1], 1.0, jnp.where(r == lp[1:2], 1.0,
             jnp.where(r == lp[2:3], 1.0, jnp.where(r == lp[3:4], 1.0, 0.0))))
    sorted_ref[slot] = jnp.dot(onehot.astype(BF16), hn_ref[...].astype(BF16),
                               preferred_element_type=F32)

    def runs(tile, buf_slot, wait):
        def one_expert(e, off):
            n = cnt_ref[tile * N_EXPERTS + e]
            dst = row_ref[tile * N_EXPERTS + e]
            def one(sub, size):
                cp = _rows_copy(sorted_ref.at[buf_slot], off + sub, xs_ref, dst + sub, size,
                                sems.at[buf_slot])
                cp.wait() if wait else cp.start()
            _for_row_chunks(n, one)
            return off + n
        lax.fori_loop(0, N_EXPERTS, one_expert, 0)

    runs(t, slot, False)

    @pl.when(t > 0)
    def _():
        runs(t - 1, 1 - slot, True)

    @pl.when(t == nt - 1)
    def _():
        runs(t, slot, True)


def _dispatch(run_row, run_cnt, pad_row, pad_cnt, lpos, hn, m_pad):
    T = hn.shape[0]
    tm = min(ROUTE_TILE, T)
    return pl.pallas_call(
        _dispatch_body,
        grid_spec=pltpu.PrefetchScalarGridSpec(
            num_scalar_prefetch=4,
            grid=(T // tm,),
            in_specs=[
                pl.BlockSpec((TOP_K, tm), lambda i, *_: (0, i)),
                pl.BlockSpec((tm, D_MODEL), lambda i, *_: (i, 0)),
            ],
            out_specs=pl.BlockSpec(memory_space=pl.ANY),
            scratch_shapes=[
                pltpu.VMEM((2, _sort_rows(tm), D_MODEL), F32),
                pltpu.VMEM((ROW_CHUNKS[-1], D_MODEL), F32),
                pltpu.SemaphoreType.DMA((2,)),
                pltpu.SemaphoreType.DMA(()),
            ],
        ),
        out_shape=jax.ShapeDtypeStruct((m_pad, D_MODEL), F32),
        compiler_params=_params("arbitrary"),
        name="dispatch",
    )(run_row, run_cnt, pad_row, pad_cnt, lpos, hn)


def _ffn_body(te_ref, nu_ref, xs_ref, wgu_ref, bgu_ref, wd_ref, bd_ref, ys_ref):
    del te_ref

    @pl.when(pl.program_id(0) < nu_ref[0])
    def _():
        gu = jnp.dot(xs_ref[...].astype(BF16), wgu_ref[...], preferred_element_type=F32) + bgu_ref[...]
        glu = jnp.minimum(gu[:, :D_FF], SWIGLU_LIMIT)
        lin = jnp.clip(gu[:, D_FF:], -SWIGLU_LIMIT, SWIGLU_LIMIT)
        act = glu * _sigmoid(SWIGLU_ALPHA * glu) * (lin + 1.0)
        ys_ref[...] = jnp.dot(act.astype(BF16), wd_ref[...], preferred_element_type=F32) + bd_ref[...]


def _ffn(tile_expert, n_used, xs, wgu, bgu, wd, bd):
    M_pad = xs.shape[0]
    tm = MOE_TILE
    n_tiles = M_pad // tm
    row = lambda i, te, nu: (jnp.minimum(i, nu[0] - 1), 0)
    return pl.pallas_call(
        _ffn_body,
        grid_spec=pltpu.PrefetchScalarGridSpec(
            num_scalar_prefetch=2,
            grid=(n_tiles,),
            in_specs=[
                pl.BlockSpec((tm, D_MODEL), row),
                pl.BlockSpec((None, D_MODEL, 2 * D_FF), lambda i, te, nu: (te[i], 0, 0)),
                pl.BlockSpec((None, 1, 2 * D_FF), lambda i, te, nu: (te[i], 0, 0)),
                pl.BlockSpec((None, D_FF, D_MODEL), lambda i, te, nu: (te[i], 0, 0)),
                pl.BlockSpec((None, 1, D_MODEL), lambda i, te, nu: (te[i], 0, 0)),
            ],
            out_specs=pl.BlockSpec((tm, D_MODEL), row),
        ),
        out_shape=jax.ShapeDtypeStruct((M_pad, D_MODEL), F32),
        compiler_params=_params("arbitrary"),
        name="expert_ffn",
    )(tile_expert, n_used, xs, wgu, bgu, wd, bd)


def _combine_body(row_ref, cnt_ref, lpos_ref, w_ref, h1_ref, ys_ref, o_ref, buf_ref, sems):
    t = pl.program_id(0)
    nt = pl.num_programs(0)
    tm = h1_ref.shape[0]
    slot = t % 2

    def runs(tile, buf_slot, wait):
        def one_expert(e, off):
            n = cnt_ref[tile * N_EXPERTS + e]
            src = row_ref[tile * N_EXPERTS + e]
            def one(sub, size):
                cp = _rows_copy(ys_ref, src + sub, buf_ref.at[buf_slot], off + sub, size,
                                sems.at[buf_slot])
                cp.wait() if wait else cp.start()
            _for_row_chunks(n, one)
            return off + n
        lax.fori_loop(0, N_EXPERTS, one_expert, 0)

    @pl.when(t == 0)
    def _():
        buf_ref[...] = jnp.zeros_like(buf_ref)
        runs(t, slot, False)

    @pl.when(t + 1 < nt)
    def _():
        runs(t + 1, 1 - slot, False)

    runs(t, slot, True)

    yb = buf_ref[slot].astype(BF16)
    col = lax.broadcasted_iota(I32, (tm, _sort_rows(tm)), 1)
    lp = lpos_ref[...]
    w = w_ref[...]
    out = h1_ref[...]
    for k in range(TOP_K):
        onehot = jnp.where(col == lp[:, k:k + 1], 1.0, 0.0).astype(BF16)
        out = out + w[:, k:k + 1] * jnp.dot(onehot, yb, preferred_element_type=F32)
    o_ref[...] = out


def _combine(run_row, run_cnt, lpos_tok, w_tok, h1, ys):
    T = h1.shape[0]
    tm = min(ROUTE_TILE, T)
    return pl.pallas_call(
        _combine_body,
        grid_spec=pltpu.PrefetchScalarGridSpec(
            num_scalar_prefetch=2,
            grid=(T // tm,),
            in_specs=[
                pl.BlockSpec((tm, TOP_K), lambda i, *_: (i, 0)),
                pl.BlockSpec((tm, TOP_K), lambda i, *_: (i, 0)),
                pl.BlockSpec((tm, D_MODEL), lambda i, *_: (i, 0)),
                pl.BlockSpec(memory_space=pl.ANY),
            ],
            out_specs=pl.BlockSpec((tm, D_MODEL), lambda i, *_: (i, 0)),
            scratch_shapes=[
                pltpu.VMEM((2, _sort_rows(tm), D_MODEL), F32),
                pltpu.SemaphoreType.DMA((2,)),
            ],
        ),
        out_shape=jax.ShapeDtypeStruct((T, D_MODEL), F32),
        compiler_params=_params("arbitrary"),
        name="combine",
    )(run_row, run_cnt, lpos_tok, w_tok, h1, ys)


def _layer(h, l, p):
    B, S, _ = h.shape
    T = B * S
    lambda_init = 0.8 - 0.6 * math.exp(-0.3 * l)
    hf = h.reshape(T, D_MODEL)

    w_in = p["w_in"]
    sizes = [ATTN_WIDTH, ATTN_WIDTH, ATTN_WIDTH, D_INNER, CONV_DIM, SSM_HEADS, 2 * D_MODEL]
    offs = [0]
    for s in sizes:
        offs.append(offs[-1] + s)
    wq, wk, wv, wz, wxbc, wdt, wg = (w_in[:, offs[n]:offs[n + 1]] for n in range(7))
    w_main = jnp.concatenate([wq, wk, wv, wxbc, wz, wg], axis=1).astype(BF16)
    pad = jnp.zeros((D_MODEL, LANES - 3 * SSM_HEADS), F32)
    w_dt = jnp.concatenate([wdt, wdt, wdt, pad], axis=1).astype(BF16)

    proj, dt_raw = _in_proj(hf, p["norm1_w"].reshape(1, D_MODEL), w_main, w_dt)

    qk_scale = ATTN_HEAD_DIM ** -0.5 * LOG2E
    qg = (jnp.tile(p["q_norm_w"], 2) * qk_scale).reshape(1, LANES)
    kg = jnp.tile(p["k_norm_w"], 2).reshape(1, LANES)
    slopes = jnp.power(2.0, -8.0 * (jnp.arange(ATTN_HEADS, dtype=F32) + 1.0) / ATTN_HEADS) * LOG2E
    c_tab = jnp.broadcast_to(slopes[:, None, None], (ATTN_HEADS, 1, LANES))
    c_row_tab = jnp.broadcast_to(slopes[:, None, None], (ATTN_HEADS, 1, ATTN_TILE))
    lam = (jnp.exp(jnp.sum(p["lambda_q1"] * p["lambda_k1"]))
           - jnp.exp(jnp.sum(p["lambda_q2"] * p["lambda_k2"])) + lambda_init).reshape(1)
    qT, ka, vT = _attn_prep(proj, qg, kg, c_tab, B, S)
    attn = _attention(lam, qT, ka, vT, c_row_tab, p["subln_w"].reshape(LANES, 1), B, S, lambda_init)

    rep3 = lambda v: jnp.concatenate([v, v, v, jnp.zeros((LANES - 3 * SSM_HEADS,), F32)]).reshape(1, LANES)
    head_of_row = jnp.arange(LANES) % SSM_HEADS
    head_of_col = jnp.arange(D_INNER) // SSM_HEAD_DIM
    expand = ((head_of_row[:, None] == head_of_col[None, :])
              & (jnp.arange(LANES)[:, None] < 3 * SSM_HEADS)).astype(BF16)
    ssm = _ssd(proj, dt_raw, p["conv_w"], p["conv_b"].reshape(1, CONV_DIM), rep3(p["dt_bias"]),
               rep3(p["a_log"]), jnp.repeat(p["d_skip"], SSM_HEAD_DIM).reshape(1, D_INNER),
               p["ssm_norm_w"].reshape(1, D_INNER), expand, B, S)

    wr = jnp.pad(p["w_router"], ((0, 0), (0, LANES - N_EXPERTS))).astype(BF16)
    br = jnp.pad(p["b_router"], (0, LANES - N_EXPERTS)).reshape(1, LANES)
    h1, hn, logits = _merge(attn, ssm, proj, hf, p["w_attn_proj"].astype(BF16),
                            p["w_ssm_proj"].astype(BF16), p["w_out"].astype(BF16),
                            p["norm2_w"].reshape(1, D_MODEL), wr, br)

    idx, w_top, lpos = _route(logits)
    tr = min(ROUTE_TILE, T)
    experts = jnp.arange(N_EXPERTS, dtype=I32)
    chosen = idx.reshape(TOP_K, T // tr, tr)[..., None] == experts
    tile_cnt = jnp.sum(chosen, axis=(0, 2), dtype=I32)
    tile_cnt = (tile_cnt + ROW_ALIGN - 1) // ROW_ALIGN * ROW_ALIGN
    counts = jnp.sum(tile_cnt, axis=0)
    tiles_per = (counts + MOE_TILE - 1) // MOE_TILE
    tile_end = jnp.cumsum(tiles_per)
    group_start = (tile_end - tiles_per) * MOE_TILE
    n_runs = (T // tr) * N_EXPERTS
    n_tiles = -(-(T * TOP_K + n_runs * (ROW_ALIGN - 1)) // MOE_TILE) + N_EXPERTS
    M_pad = n_tiles * MOE_TILE
    tile_expert = jnp.minimum(jnp.sum(jnp.arange(n_tiles)[:, None] >= tile_end[None, :], axis=1),
                              N_EXPERTS - 1).astype(I32)
    n_used = tile_end[-1:].astype(I32)
    run_row = (group_start[None, :] + jnp.cumsum(tile_cnt, axis=0) - tile_cnt).reshape(-1)
    run_cnt = tile_cnt.reshape(-1)

    xs = _dispatch(run_row, run_cnt, group_start + counts, tiles_per * MOE_TILE - counts,
                   lpos, hn, M_pad)
    ys = _ffn(tile_expert, n_used, xs, p["w_gate_up"].astype(BF16),
              p["b_gate_up"].reshape(N_EXPERTS, 1, 2 * D_FF), p["w_down"].astype(BF16),
              p["b_down"].reshape(N_EXPERTS, 1, D_MODEL))
    out = _combine(run_row, run_cnt, lpos.T, w_top.T, h1, ys)
    return out.reshape(B, S, D_MODEL)


def kernel(x, norm1_w, w_in, q_norm_w, k_norm_w, lambda_q1, lambda_k1, lambda_q2, lambda_k2, subln_w, conv_w, conv_b, dt_bias, a_log, d_skip, ssm_norm_w, w_attn_proj, w_ssm_proj, w_out, norm2_w, w_router, b_router, w_gate_up, b_gate_up, w_down, b_down):
    params = dict(norm1_w=norm1_w, w_in=w_in, q_norm_w=q_norm_w, k_norm_w=k_norm_w,
                  lambda_q1=lambda_q1, lambda_k1=lambda_k1, lambda_q2=lambda_q2, lambda_k2=lambda_k2,
                  subln_w=subln_w, conv_w=conv_w, conv_b=conv_b, dt_bias=dt_bias, a_log=a_log,
                  d_skip=d_skip, ssm_norm_w=ssm_norm_w, w_attn_proj=w_attn_proj,
                  w_ssm_proj=w_ssm_proj, w_out=w_out, norm2_w=norm2_w, w_router=w_router,
                  b_router=b_router, w_gate_up=w_gate_up, b_gate_up=b_gate_up, w_down=w_down,
                  b_down=b_down)
    h = x
    for l in range(w_in.shape[0]):
        h = _layer(h, l, {k: v[l] for k, v in params.items()})
    return h
```

```python
import functools
import math

import jax
import jax.numpy as jnp
from jax import lax
from jax.experimental import pallas as pl
from jax.experimental.pallas import tpu as pltpu

F32, BF16, I32, U32 = jnp.float32, jnp.bfloat16, jnp.int32, jnp.uint32

D_MODEL = 1024
ATTN_HEADS = 8
ATTN_HEAD_DIM = 64
ATTN_WIDTH = ATTN_HEADS * 2 * ATTN_HEAD_DIM
SSM_EXPAND = 2
D_INNER = SSM_EXPAND * D_MODEL
SSM_HEAD_DIM = 64
SSM_HEADS = D_INNER // SSM_HEAD_DIM
SSM_GROUPS = 4
SSM_STATE = 128
CONV_WIDTH = 4
CONV_DIM = D_INNER + 2 * SSM_GROUPS * SSM_STATE
SSM_CHUNK = 128
N_EXPERTS = 32
TOP_K = 4
D_FF = D_MODEL
SWIGLU_LIMIT = 7.0
SWIGLU_ALPHA = 1.702
EPS = 1e-5
QK_EPS = 1e-6

LANES = 128
HEADS_PER_GROUP = SSM_HEADS // SSM_GROUPS
GROUP_WIDTH = D_INNER // SSM_GROUPS
LOG2E = math.log2(math.e)
NEG = -1e30
VMEM_LIMIT = 56 * 1024 * 1024

COL_Q, COL_K, COL_V = 0, ATTN_WIDTH, 2 * ATTN_WIDTH
COL_XBC = 3 * ATTN_WIDTH
COL_Z = COL_XBC + CONV_DIM
COL_GATE = COL_Z + D_INNER
PROJ_W = COL_GATE + 2 * D_MODEL

ATTN_TILE = 512
ATTN_HEADS_PER_STEP = 2
VT_ROWS = 2 * ATTN_HEAD_DIM + 16
MOE_TILE = 512
ROUTE_TILE = 256
CONV_HALO = 16
PACKED_W = D_MODEL // 2
ROW_ALIGN = 8
ROW_CHUNKS = (8, 16, 32, 64, 128, 256)


def _sort_rows(tm):
    return TOP_K * tm + N_EXPERTS * ROW_ALIGN


def _params(*sem):
    return pltpu.CompilerParams(dimension_semantics=sem, vmem_limit_bytes=VMEM_LIMIT)


def _sigmoid(x):
    return 0.5 * jnp.tanh(0.5 * x) + 0.5


def _silu(x):
    h = 0.5 * x
    return h + h * jnp.tanh(h)


def _pack_bf16_pairs(x):
    w = x.shape[1] // 2
    lo = lax.bitcast_convert_type(x[:, :w], U32) >> 16
    hi = lax.bitcast_convert_type(x[:, w:], U32) & jnp.uint32(0xFFFF0000)
    return hi | lo


def _unpack_bf16_pairs(p):
    lo = lax.bitcast_convert_type(p << 16, F32)
    hi = lax.bitcast_convert_type(p & jnp.uint32(0xFFFF0000), F32)
    return jnp.concatenate([lo, hi], axis=1).astype(BF16)


def _inproj_body(x_ref, g_ref, w_ref, wdt_ref, o_ref, dt_ref, u_scr):
    @pl.when(pl.program_id(1) == 0)
    def _():
        x = x_ref[...]
        u = x * lax.rsqrt(jnp.mean(x * x, axis=-1, keepdims=True) + EPS) * g_ref[...]
        ub = u.astype(BF16)
        u_scr[...] = ub
        dt_ref[...] = jnp.dot(ub, wdt_ref[...], preferred_element_type=F32)

    o_ref[...] = jnp.dot(u_scr[...], w_ref[...], preferred_element_type=F32).astype(o_ref.dtype)


def _in_proj(xf, gain, w_main, w_dt):
    T = xf.shape[0]
    tm = min(1024, T)
    tn = 2048
    return pl.pallas_call(
        _inproj_body,
        grid=(T // tm, PROJ_W // tn),
        in_specs=[
            pl.BlockSpec((tm, D_MODEL), lambda i, j: (i, 0)),
            pl.BlockSpec((1, D_MODEL), lambda i, j: (0, 0)),
            pl.BlockSpec((D_MODEL, tn), lambda i, j: (0, j)),
            pl.BlockSpec((D_MODEL, LANES), lambda i, j: (0, 0)),
        ],
        out_specs=[
            pl.BlockSpec((tm, tn), lambda i, j: (i, j)),
            pl.BlockSpec((tm, LANES), lambda i, j: (i, 0)),
        ],
        out_shape=[
            jax.ShapeDtypeStruct((T, PROJ_W), BF16),
            jax.ShapeDtypeStruct((T, LANES), F32),
        ],
        scratch_shapes=[pltpu.VMEM((tm, D_MODEL), BF16)],
        compiler_params=_params("parallel", "arbitrary"),
        name="in_proj",
    )(xf, gain, w_main, w_dt)


def _split3(x):
    hi = x.astype(BF16).astype(F32)
    r = x - hi
    mid = r.astype(BF16).astype(F32)
    return hi, mid, r - mid


def _attn_prep_body(q_ref, k_ref, v_ref, qg_ref, kg_ref, c_ref, qT_ref, ka_ref, vT_ref, *, tk):
    ts = q_ref.shape[0]
    lane = lax.broadcasted_iota(I32, (ts, LANES), 1)
    lo_half = lane < ATTN_HEAD_DIM

    def half_norm(x, g):
        x2 = x * x
        s_lo = jnp.sum(jnp.where(lo_half, x2, 0.0), axis=-1, keepdims=True)
        s_hi = jnp.sum(jnp.where(lo_half, 0.0, x2), axis=-1, keepdims=True)
        ms = jnp.where(lo_half, s_lo, s_hi) * (1.0 / ATTN_HEAD_DIM)
        return x * lax.rsqrt(ms + QK_EPS) * g

    q = half_norm(q_ref[...].astype(F32), qg_ref[...])
    k = half_norm(k_ref[...].astype(F32), kg_ref[...])

    row = lax.broadcasted_iota(I32, (ts, LANES), 0) + pl.program_id(2) * ts
    bias = c_ref[...] * (row % tk).astype(F32)
    b_hi, b_mid, b_lo = _split3(bias)
    k_tail = jnp.where(lane == ATTN_HEAD_DIM, b_hi,
                       jnp.where(lane == ATTN_HEAD_DIM + 1, b_mid,
                                 jnp.where(lane == ATTN_HEAD_DIM + 2, b_lo, 0.0)))
    q_tail = jnp.where(lane < ATTN_HEAD_DIM + 3, 1.0, 0.0)

    for comp in range(2):
        qc = q if comp == 0 else pltpu.roll(q, ATTN_HEAD_DIM, 1)
        kc = k if comp == 0 else pltpu.roll(k, ATTN_HEAD_DIM, 1)
        qT_ref[comp] = jnp.where(lo_half, qc, q_tail).T.astype(BF16)
        ka_ref[comp] = jnp.where(lo_half, kc, k_tail).astype(BF16)
    dv = 2 * ATTN_HEAD_DIM
    vT_ref[0:dv, :] = v_ref[...].astype(F32).T.astype(BF16)
    vT_ref[dv:VT_ROWS, :] = jnp.ones((VT_ROWS - dv, ts), BF16)


def _attn_prep(proj, qg, kg, c_tab, B, S):
    ts = min(1024, S)
    ns = S // ts
    H = ATTN_HEADS
    hb = ATTN_WIDTH // LANES
    return pl.pallas_call(
        functools.partial(_attn_prep_body, tk=ATTN_TILE),
        grid=(B, H, ns),
        in_specs=[
            pl.BlockSpec((ts, LANES), lambda b, h, s: (b * ns + s, h)),
            pl.BlockSpec((ts, LANES), lambda b, h, s: (b * ns + s, hb + h)),
            pl.BlockSpec((ts, LANES), lambda b, h, s: (b * ns + s, 2 * hb + h)),
            pl.BlockSpec((1, LANES), lambda b, h, s: (0, 0)),
            pl.BlockSpec((1, LANES), lambda b, h, s: (0, 0)),
            pl.BlockSpec((None, 1, LANES), lambda b, h, s: (h, 0, 0)),
        ],
        out_specs=[
            pl.BlockSpec((None, None, 2, LANES, ts), lambda b, h, s: (b, h, 0, 0, s)),
            pl.BlockSpec((None, None, 2, ts, LANES), lambda b, h, s: (b, h, 0, s, 0)),
            pl.BlockSpec((None, None, VT_ROWS, ts), lambda b, h, s: (b, h, 0, s)),
        ],
        out_shape=[
            jax.ShapeDtypeStruct((B, H, 2, LANES, S), BF16),
            jax.ShapeDtypeStruct((B, H, 2, S, LANES), BF16),
            jax.ShapeDtypeStruct((B, H, VT_ROWS, S), BF16),
        ],
        compiler_params=_params("parallel", "parallel", "parallel"),
        name="attn_prep",
    )(proj, proj, proj, qg, kg, c_tab)


def _attn_body(lam_ref, qT_ref, ka_ref, vT_ref, c_ref, w_ref, o_ref, *scratch, tile, out_scale):
    i = pl.program_id(2)
    streams = [(h, comp) for h in range(ATTN_HEADS_PER_STEP) for comp in range(2)]
    ns = len(streams)
    slot_a, slot_b, accs = scratch[:ns], scratch[ns:2 * ns], scratch[2 * ns:]
    for acc in accs:
        acc[...] = jnp.zeros_like(acc)
    c_rows = [c_ref[h] for h in range(ATTN_HEADS_PER_STEP)]

    def scores(j, slot):
        start = pl.multiple_of(j * tile, tile)
        for n, (h, comp) in enumerate(streams):
            slot[n][...] = jnp.dot(ka_ref[h, comp, pl.ds(start, tile), :], qT_ref[h, comp],
                                   preferred_element_type=F32)

    def softmax_pv(j, slot, ms, masked):
        start = pl.multiple_of(j * tile, tile)
        new_ms = []
        for n, (h, comp) in enumerate(streams):
            cb = c_rows[h] * ((j - i) * tile).astype(F32)
            sT = slot[n][...]
            if masked:
                key = lax.broadcasted_iota(I32, sT.shape, 0)
                qry = lax.broadcasted_iota(I32, sT.shape, 1)
                sT = jnp.where(key <= qry, sT, NEG)
            m_new = jnp.maximum(ms[n], jnp.max(sT, axis=0, keepdims=True) + cb)
            alpha = jnp.exp2(ms[n] - m_new)
            p = jnp.exp2(sT - (m_new - cb)).astype(BF16)
            pv = jnp.dot(vT_ref[h, :, pl.ds(start, tile)], p, preferred_element_type=F32)
            accs[n][...] = alpha * accs[n][...] + pv
            new_ms.append(m_new)
        return tuple(new_ms)

    def finish(slot, ms):
        softmax_pv(i, slot, ms, True)
        dv = 2 * ATTN_HEAD_DIM
        for h in range(ATTN_HEADS_PER_STEP):
            outs = [accs[2 * h + comp][0:dv, :] / accs[2 * h + comp][dv:dv + 1, :]
                    for comp in range(2)]
            o = outs[0] - lam_ref[0] * outs[1]
            msq = jnp.mean(o * o, axis=0, keepdims=True)
            y = o * lax.rsqrt(msq + EPS) * w_ref[...] * out_scale
            o_ref[:, h * dv:(h + 1) * dv] = y.T.astype(o_ref.dtype)

    def body(t, ms):
        j = 2 * t
        scores(j + 1, slot_b)
        ms = softmax_pv(j, slot_a, ms, False)
        scores(j + 2, slot_a)
        return softmax_pv(j + 1, slot_b, ms, False)

    scores(0, slot_a)
    m_init = jnp.full((1, tile), NEG, F32)
    ms = lax.fori_loop(0, i // 2, body, (m_init,) * ns)

    @pl.when(i % 2 == 0)
    def _():
        finish(slot_a, ms)

    @pl.when(i % 2 == 1)
    def _():
        scores(i, slot_b)
        finish(slot_b, softmax_pv(i - 1, slot_a, ms, False))


def _attention(lam, qT, ka, vT, c_row_tab, subln_col, B, S, lambda_init):
    tile = ATTN_TILE
    nh = ATTN_HEADS_PER_STEP
    nq = S // tile
    n_streams = 2 * nh
    return pl.pallas_call(
        functools.partial(_attn_body, tile=tile, out_scale=1.0 - lambda_init),
        grid=(B, ATTN_HEADS // nh, nq),
        in_specs=[
            pl.BlockSpec(memory_space=pltpu.SMEM),
            pl.BlockSpec((None, nh, 2, LANES, tile), lambda b, g, i: (b, g, 0, 0, i)),
            pl.BlockSpec((None, nh, 2, S, LANES), lambda b, g, i: (b, g, 0, 0, 0)),
            pl.BlockSpec((None, nh, VT_ROWS, S), lambda b, g, i: (b, g, 0, 0)),
            pl.BlockSpec((nh, 1, tile), lambda b, g, i: (g, 0, 0)),
            pl.BlockSpec((LANES, 1), lambda b, g, i: (0, 0)),
        ],
        out_specs=pl.BlockSpec((tile, nh * LANES), lambda b, g, i: (b * nq + i, g)),
        out_shape=jax.ShapeDtypeStruct((B * S, ATTN_WIDTH), BF16),
        scratch_shapes=([pltpu.VMEM((tile, tile), F32)] * (2 * n_streams)
                        + [pltpu.VMEM((VT_ROWS, tile), F32)] * n_streams),
        compiler_params=_params("parallel", "parallel", "arbitrary"),
        name="diff_attn",
    )(lam, qT, ka, vT, c_row_tab, subln_col)


def _expand_heads(x, e_ref):
    lane = lax.broadcasted_iota(I32, x.shape, 1)
    hi, mid, lo = _split3(x)
    parts = jnp.where(lane < SSM_HEADS, hi,
                      jnp.where(lane < 2 * SSM_HEADS, mid,
                                jnp.where(lane < 3 * SSM_HEADS, lo, 0.0)))
    return jnp.dot(parts.astype(BF16), e_ref[...], preferred_element_type=F32)


def _ssd_body(xbc_ref, z_ref, dt_ref, cw_ref, cb_ref, dtb_ref, alog_ref, dskip_ref, nw_ref, e_ref,
              o_ref, ext_ref, state_ref, y_ref):
    Q = SSM_CHUNK
    halo = CONV_HALO

    @pl.when(pl.program_id(1) == 0)
    def _():
        ext_ref[0:halo, :] = jnp.zeros((halo, CONV_DIM), BF16)
        state_ref[...] = jnp.zeros_like(state_ref)

    xb = xbc_ref[...]
    ext_ref[halo:halo + Q, :] = xb
    taps = CONV_WIDTH - 1
    out_row = lax.broadcasted_iota(I32, (taps * Q, halo + Q), 0)
    src_row = lax.broadcasted_iota(I32, (taps * Q, halo + Q), 1)
    shift = (src_row == out_row - (out_row // Q) * (Q - 1) + (halo - taps)).astype(BF16)
    shifted = jnp.dot(shift, ext_ref[...], preferred_element_type=F32)
    conv = cb_ref[...] + cw_ref[taps:taps + 1, :] * xb.astype(F32)
    for w in range(taps):
        conv = conv + cw_ref[w:w + 1, :] * shifted[w * Q:(w + 1) * Q, :]
    ext_ref[0:halo, :] = xb[Q - halo:Q, :]
    xc = _silu(conv)
    xs = xc[:, :D_INNER]

    raw = dt_ref[...] + dtb_ref[...]
    dt = jnp.maximum(raw, 0.0) + jnp.log2(1.0 + jnp.exp(-jnp.abs(raw))) * math.log(2.0)
    dA = dt * (-jnp.exp(alog_ref[...]))
    t_idx = lax.broadcasted_iota(I32, (Q, Q), 0)
    s_idx = lax.broadcasted_iota(I32, (Q, Q), 1)
    causal = s_idx <= t_idx
    cum = jnp.dot(causal.astype(F32), dA, preferred_element_type=F32,
                  precision=lax.Precision.HIGHEST)
    cumT = cum.T
    ecum = jnp.exp(cum)
    decay = jnp.exp(cum[Q - 1:Q, :] - cum)

    dt_e = _expand_heads(dt, e_ref)
    ecum_e = _expand_heads(ecum, e_ref)
    decay_e = _expand_heads(decay, e_ref)
    xdt = xs * dt_e
    xdt_b = xdt.astype(BF16)
    xdec_b = (xdt * decay_e).astype(BF16)

    lane = lax.broadcasted_iota(I32, (Q, LANES), 1)
    first_head = lane < SSM_HEAD_DIM
    for g in range(SSM_GROUPS):
        bcol = D_INNER + g * SSM_STATE
        ccol = D_INNER + SSM_GROUPS * SSM_STATE + g * SSM_STATE
        Bg = xc[:, bcol:bcol + SSM_STATE].astype(BF16)
        Cg = xc[:, ccol:ccol + SSM_STATE].astype(BF16)
        gs = slice(g * GROUP_WIDTH, (g + 1) * GROUP_WIDTH)
        CB = lax.dot_general(Cg, Bg, (((1,), (1,)), ((), ())), preferred_element_type=F32)
        state = state_ref[g]
        y_off = jnp.dot(Cg, state.astype(BF16), preferred_element_type=F32) * ecum_e[:, gs]
        for pair in range(HEADS_PER_GROUP // 2):
            h0 = g * HEADS_PER_GROUP + 2 * pair
            ms = []
            for h in (h0, h0 + 1):
                diff = cum[:, h:h + 1] - cumT[h:h + 1, :]
                L = jnp.exp(jnp.where(causal, diff, NEG))
                ms.append((CB * L).astype(BF16))
            lhs = jnp.concatenate(ms, axis=1)
            c0 = h0 * SSM_HEAD_DIM
            xp = xdt_b[:, c0:c0 + LANES]
            zero = jnp.zeros_like(xp)
            rhs = jnp.concatenate([jnp.where(first_head, xp, zero),
                                   jnp.where(first_head, zero, xp)], axis=0)
            y_ref[:, c0:c0 + LANES] = (jnp.dot(lhs, rhs, preferred_element_type=F32)
                                       + y_off[:, c0 - g * GROUP_WIDTH:c0 - g * GROUP_WIDTH + LANES])
        upd = lax.dot_general(Bg, xdec_b[:, gs], (((0,), (0,)), ((), ())),
                              preferred_element_type=F32)
        state_ref[g] = state * ecum_e[Q - 1:Q, gs] + upd

    y = y_ref[...] + xs * dskip_ref[...]
    z = z_ref[...].astype(F32)
    y = y * _silu(z)
    for g in range(SSM_GROUPS):
        gs = slice(g * GROUP_WIDTH, (g + 1) * GROUP_WIDTH)
        yg = y[:, gs]
        ms = jnp.mean(yg * yg, axis=-1, keepdims=True)
        o_ref[:, gs] = (yg * lax.rsqrt(ms + EPS) * nw_ref[:, gs]).astype(o_ref.dtype)


def _ssd(proj, dt_raw, conv_w, conv_b, dtb3, alog3, dskip_e, norm_w, expand, B, S):
    Q = SSM_CHUNK
    nc = S // Q
    row = lambda b, c: b * nc + c
    const = lambda b, c: (0, 0)
    return pl.pallas_call(
        _ssd_body,
        grid=(B, nc),
        in_specs=[
            pl.BlockSpec((Q, CONV_DIM), lambda b, c: (row(b, c), COL_XBC // CONV_DIM)),
            pl.BlockSpec((Q, D_INNER), lambda b, c: (row(b, c), COL_Z // D_INNER)),
            pl.BlockSpec((Q, LANES), lambda b, c: (row(b, c), 0)),
            pl.BlockSpec((CONV_WIDTH, CONV_DIM), const),
            pl.BlockSpec((1, CONV_DIM), const),
            pl.BlockSpec((1, LANES), const),
            pl.BlockSpec((1, LANES), const),
            pl.BlockSpec((1, D_INNER), const),
            pl.BlockSpec((1, D_INNER), const),
            pl.BlockSpec((LANES, D_INNER), const),
        ],
        out_specs=pl.BlockSpec((Q, D_INNER), lambda b, c: (row(b, c), 0)),
        out_shape=jax.ShapeDtypeStruct((B * S, D_INNER), BF16),
        scratch_shapes=[
            pltpu.VMEM((Q + CONV_HALO, CONV_DIM), BF16),
            pltpu.VMEM((SSM_GROUPS, SSM_STATE, GROUP_WIDTH), F32),
            pltpu.VMEM((Q, D_INNER), F32),
        ],
        compiler_params=_params("parallel", "arbitrary"),
        name="ssd",
    )(proj, proj, dt_raw, conv_w, conv_b, dtb3, alog3, dskip_e, norm_w, expand)


def _merge_body(attn_ref, ssm_ref, gate_ref, x_ref, wa_ref, ws_ref, wo_ref, n2_ref, wr_ref, br_ref,
                h1_ref, hn_ref, lg_ref):
    gate = gate_ref[...].astype(F32)
    ya = jnp.dot(attn_ref[...], wa_ref[...], preferred_element_type=F32)
    ys = jnp.dot(ssm_ref[...], ws_ref[...], preferred_element_type=F32)
    mixed = _sigmoid(gate[:, :D_MODEL]) * ya + _sigmoid(gate[:, D_MODEL:]) * ys
    h1 = x_ref[...] + jnp.dot(mixed.astype(BF16), wo_ref[...], preferred_element_type=F32)
    h1_ref[...] = h1
    hn = h1 * lax.rsqrt(jnp.mean(h1 * h1, axis=-1, keepdims=True) + EPS) * n2_ref[...]
    hn_ref[...] = hn
    lg_ref[...] = jnp.dot(hn.astype(BF16), wr_ref[...], preferred_element_type=F32) + br_ref[...]


def _merge(attn, ssm, proj, xf, wa, ws, wo, n2, wr, br):
    T = xf.shape[0]
    tm = min(256, T)
    const = lambda i: (0, 0)
    return pl.pallas_call(
        _merge_body,
        grid=(T // tm,),
        in_specs=[
            pl.BlockSpec((tm, ATTN_WIDTH), lambda i: (i, 0)),
            pl.BlockSpec((tm, D_INNER), lambda i: (i, 0)),
            pl.BlockSpec((tm, 2 * D_MODEL), lambda i: (i, COL_GATE // (2 * D_MODEL))),
            pl.BlockSpec((tm, D_MODEL), lambda i: (i, 0)),
            pl.BlockSpec((ATTN_WIDTH, D_MODEL), const),
            pl.BlockSpec((D_INNER, D_MODEL), const),
            pl.BlockSpec((D_MODEL, D_MODEL), const),
            pl.BlockSpec((1, D_MODEL), const),
            pl.BlockSpec((D_MODEL, LANES), const),
            pl.BlockSpec((1, LANES), const),
        ],
        out_specs=[
            pl.BlockSpec((tm, D_MODEL), lambda i: (i, 0)),
            pl.BlockSpec((tm, D_MODEL), lambda i: (i, 0)),
            pl.BlockSpec((tm, LANES), lambda i: (i, 0)),
        ],
        out_shape=[
            jax.ShapeDtypeStruct((T, D_MODEL), F32),
            jax.ShapeDtypeStruct((T, D_MODEL), F32),
            jax.ShapeDtypeStruct((T, LANES), F32),
        ],
        compiler_params=_params("parallel"),
        name="merge",
    )(attn, ssm, proj, xf, wa, ws, wo, n2, wr, br)


def _route_body(lg_ref, idx_ref, w_ref, lpos_ref):
    tm = lg_ref.shape[0]
    logit = lg_ref[...].T[:N_EXPERTS, :]
    eio = lax.broadcasted_iota(I32, logit.shape, 0)
    vals, hits = [], []
    for k in range(TOP_K):
        mx = jnp.max(logit, axis=0, keepdims=True)
        idx = jnp.min(jnp.where(logit == mx, eio, N_EXPERTS), axis=0, keepdims=True)
        hit = eio == idx
        logit = jnp.where(hit, -jnp.inf, logit)
        idx_ref[k:k + 1, :] = idx
        vals.append(mx)
        hits.append(hit)
    exps = [jnp.exp(v - vals[0]) for v in vals]
    denom = exps[0] + exps[1] + exps[2] + exps[3]
    for k in range(TOP_K):
        w_ref[k:k + 1, :] = exps[k] / denom

    sel = (hits[0] | hits[1] | hits[2] | hits[3]).astype(BF16)
    before = (lax.broadcasted_iota(I32, (tm, tm), 0) < lax.broadcasted_iota(I32, (tm, tm), 1))
    prefix = jnp.dot(sel, before.astype(BF16), preferred_element_type=F32)
    count = jnp.sum(sel.astype(F32), axis=1, keepdims=True)
    count = jnp.floor((count + (ROW_ALIGN - 1)) * (1.0 / ROW_ALIGN)) * ROW_ALIGN
    count = jnp.broadcast_to(count, (N_EXPERTS, LANES))
    lower = (lax.broadcasted_iota(I32, (N_EXPERTS, N_EXPERTS), 1)
             < lax.broadcasted_iota(I32, (N_EXPERTS, N_EXPERTS), 0))
    start = jnp.dot(lower.astype(BF16), count.astype(BF16), preferred_element_type=F32)[:, 0:1]
    offs = prefix + start
    for k in range(TOP_K):
        lpos_ref[k:k + 1, :] = jnp.sum(jnp.where(hits[k], offs, 0.0), axis=0,
                                       keepdims=True).astype(I32)


def _route(logits):
    T = logits.shape[0]
    tm = min(ROUTE_TILE, T)
    spec = pl.BlockSpec((TOP_K, tm), lambda i: (0, i))
    return pl.pallas_call(
        _route_body,
        grid=(T // tm,),
        in_specs=[pl.BlockSpec((tm, LANES), lambda i: (i, 0))],
        out_specs=[spec, spec, spec],
        out_shape=[
            jax.ShapeDtypeStruct((TOP_K, T), I32),
            jax.ShapeDtypeStruct((TOP_K, T), F32),
            jax.ShapeDtypeStruct((TOP_K, T), I32),
        ],
        compiler_params=_params("parallel"),
        name="route",
    )(logits)


def _for_row_chunks(count, fn):
    for size in ROW_CHUNKS:
        @pl.when((count & size) != 0)
        def _():
            fn(count & (size - 1), size)


def _rows_copy(src_ref, src_row, dst_ref, dst_row, size, sem):
    return pltpu.make_async_copy(src_ref.at[pl.ds(pl.multiple_of(src_row, ROW_ALIGN), size), :],
                                 dst_ref.at[pl.ds(pl.multiple_of(dst_row, ROW_ALIGN), size), :], sem)


def _dispatch_body(row_ref, cnt_ref, pad_row_ref, pad_cnt_ref, lpos_ref, hn_ref, xs_ref,
                   sorted_ref, zero_ref, sems, pad_sem):
    t = pl.program_id(0)
    nt = pl.num_programs(0)
    tm = hn_ref.shape[0]
    slot = t % 2

    @pl.when(t == 0)
    def _():
        zero_ref[...] = jnp.zeros_like(zero_ref)
        for wait in (False, True):
            def pad(e, carry):
                def one(off, size):
                    cp = _rows_copy(zero_ref, 0, xs_ref, pad_row_ref[e] + off, size, pad_sem)
                    cp.wait() if wait else cp.start()
                _for_row_chunks(pad_cnt_ref[e], one)
                return carry
            lax.fori_loop(0, N_EXPERTS, pad, 0)

    lp = lpos_ref[...]
    r = lax.broadcasted_iota(I32, (_sort_rows(tm), tm), 0)
    onehot = jnp.where(r == lp[0:1], 1.0, jnp.where(r == lp[1:2], 1.0,
             jnp.where(r == lp[2:3], 1.0, jnp.where(r == lp[3:4], 1.0, 0.0))))
    sorted_ref[slot] = _pack_bf16_pairs(jnp.dot(onehot.astype(BF16), hn_ref[...].astype(BF16),
                                                preferred_element_type=F32))

    def runs(tile, buf_slot, wait):
        def one_expert(e, off):
            n = cnt_ref[tile * N_EXPERTS + e]
            dst = row_ref[tile * N_EXPERTS + e]
            def one(sub, size):
                cp = _rows_copy(sorted_ref.at[buf_slot], off + sub, xs_ref, dst + sub, size,
                                sems.at[buf_slot])
                cp.wait() if wait else cp.start()
            _for_row_chunks(n, one)
            return off + n
        lax.fori_loop(0, N_EXPERTS, one_expert, 0)

    runs(t, slot, False)

    @pl.when(t > 0)
    def _():
        runs(t - 1, 1 - slot, True)

    @pl.when(t == nt - 1)
    def _():
        runs(t, slot, True)


def _dispatch(run_row, run_cnt, pad_row, pad_cnt, lpos, hn, m_pad):
    T = hn.shape[0]
    tm = min(ROUTE_TILE, T)
    return pl.pallas_call(
        _dispatch_body,
        grid_spec=pltpu.PrefetchScalarGridSpec(
            num_scalar_prefetch=4,
            grid=(T // tm,),
            in_specs=[
                pl.BlockSpec((TOP_K, tm), lambda i, *_: (0, i)),
                pl.BlockSpec((tm, D_MODEL), lambda i, *_: (i, 0)),
            ],
            out_specs=pl.BlockSpec(memory_space=pl.ANY),
            scratch_shapes=[
                pltpu.VMEM((2, _sort_rows(tm), PACKED_W), U32),
                pltpu.VMEM((ROW_CHUNKS[-1], PACKED_W), U32),
                pltpu.SemaphoreType.DMA((2,)),
                pltpu.SemaphoreType.DMA(()),
            ],
        ),
        out_shape=jax.ShapeDtypeStruct((m_pad, PACKED_W), U32),
        compiler_params=_params("arbitrary"),
        name="dispatch",
    )(run_row, run_cnt, pad_row, pad_cnt, lpos, hn)


def _ffn_body(te_ref, nu_ref, xs_ref, wgu_ref, bgu_ref, wd_ref, bd_ref, ys_ref):
    del te_ref

    @pl.when(pl.program_id(0) < nu_ref[0])
    def _():
        gu = jnp.dot(_unpack_bf16_pairs(xs_ref[...]), wgu_ref[...],
                     preferred_element_type=F32) + bgu_ref[...]
        glu = jnp.minimum(gu[:, :D_FF], SWIGLU_LIMIT)
        lin = jnp.clip(gu[:, D_FF:], -SWIGLU_LIMIT, SWIGLU_LIMIT)
        act = glu * _sigmoid(SWIGLU_ALPHA * glu) * (lin + 1.0)
        y = jnp.dot(act.astype(BF16), wd_ref[...], preferred_element_type=F32) + bd_ref[...]
        ys_ref[...] = _pack_bf16_pairs(y.astype(BF16).astype(F32))


def _ffn(tile_expert, n_used, xs, wgu, bgu, wd, bd):
    M_pad = xs.shape[0]
    tm = MOE_TILE
    n_tiles = M_pad // tm
    row = lambda i, te, nu: (jnp.minimum(i, nu[0] - 1), 0)
    return pl.pallas_call(
        _ffn_body,
        grid_spec=pltpu.PrefetchScalarGridSpec(
            num_scalar_prefetch=2,
            grid=(n_tiles,),
            in_specs=[
                pl.BlockSpec((tm, PACKED_W), row),
                pl.BlockSpec((None, D_MODEL, 2 * D_FF), lambda i, te, nu: (te[i], 0, 0)),
                pl.BlockSpec((None, 1, 2 * D_FF), lambda i, te, nu: (te[i], 0, 0)),
                pl.BlockSpec((None, D_FF, D_MODEL), lambda i, te, nu: (te[i], 0, 0)),
                pl.BlockSpec((None, 1, D_MODEL), lambda i, te, nu: (te[i], 0, 0)),
            ],
            out_specs=pl.BlockSpec((tm, PACKED_W), row),
        ),
        out_shape=jax.ShapeDtypeStruct((M_pad, PACKED_W), U32),
        compiler_params=_params("arbitrary"),
        name="expert_ffn",
    )(tile_expert, n_used, xs, wgu, bgu, wd, bd)


def _combine_body(row_ref, cnt_ref, lpos_ref, w_ref, h1_ref, ys_ref, o_ref, buf_ref, sems):
    t = pl.program_id(0)
    nt = pl.num_programs(0)
    tm = h1_ref.shape[0]
    slot = t % 2

    def runs(tile, buf_slot, wait):
        def one_expert(e, off):
            n = cnt_ref[tile * N_EXPERTS + e]
            src = row_ref[tile * N_EXPERTS + e]
            def one(sub, size):
                cp = _rows_copy(ys_ref, src + sub, buf_ref.at[buf_slot], off + sub, size,
                                sems.at[buf_slot])
                cp.wait() if wait else cp.start()
            _for_row_chunks(n, one)
            return off + n
        lax.fori_loop(0, N_EXPERTS, one_expert, 0)

    @pl.when(t == 0)
    def _():
        buf_ref[...] = jnp.zeros_like(buf_ref)
        runs(t, slot, False)

    @pl.when(t + 1 < nt)
    def _():
        runs(t + 1, 1 - slot, False)

    runs(t, slot, True)

    yb = _unpack_bf16_pairs(buf_ref[slot])
    col = lax.broadcasted_iota(I32, (tm, _sort_rows(tm)), 1)
    lp = lpos_ref[...]
    w = w_ref[...]
    wmat = jnp.zeros(col.shape, F32)
    for k in range(TOP_K):
        wmat = jnp.where(col == lp[:, k:k + 1], w[:, k:k + 1], wmat)
    w_hi = wmat.astype(BF16)
    w_lo = (wmat - w_hi.astype(F32)).astype(BF16)
    o_ref[...] = (h1_ref[...] + jnp.dot(w_hi, yb, preferred_element_type=F32)
                  + jnp.dot(w_lo, yb, preferred_element_type=F32))


def _combine(run_row, run_cnt, lpos_tok, w_tok, h1, ys):
    T = h1.shape[0]
    tm = min(ROUTE_TILE, T)
    return pl.pallas_call(
        _combine_body,
        grid_spec=pltpu.PrefetchScalarGridSpec(
            num_scalar_prefetch=2,
            grid=(T // tm,),
            in_specs=[
                pl.BlockSpec((tm, TOP_K), lambda i, *_: (i, 0)),
                pl.BlockSpec((tm, TOP_K), lambda i, *_: (i, 0)),
                pl.BlockSpec((tm, D_MODEL), lambda i, *_: (i, 0)),
                pl.BlockSpec(memory_space=pl.ANY),
            ],
            out_specs=pl.BlockSpec((tm, D_MODEL), lambda i, *_: (i, 0)),
            scratch_shapes=[
                pltpu.VMEM((2, _sort_rows(tm), PACKED_W), U32),
                pltpu.SemaphoreType.DMA((2,)),
            ],
        ),
        out_shape=jax.ShapeDtypeStruct((T, D_MODEL), F32),
        compiler_params=_params("arbitrary"),
        name="combine",
    )(run_row, run_cnt, lpos_tok, w_tok, h1, ys)


def _layer(h, l, p):
    B, S, _ = h.shape
    T = B * S
    lambda_init = 0.8 - 0.6 * math.exp(-0.3 * l)
    hf = h.reshape(T, D_MODEL)

    w_in = p["w_in"]
    sizes = [ATTN_WIDTH, ATTN_WIDTH, ATTN_WIDTH, D_INNER, CONV_DIM, SSM_HEADS, 2 * D_MODEL]
    offs = [0]
    for s in sizes:
        offs.append(offs[-1] + s)
    wq, wk, wv, wz, wxbc, wdt, wg = (w_in[:, offs[n]:offs[n + 1]] for n in range(7))
    w_main = jnp.concatenate([wq, wk, wv, wxbc, wz, wg], axis=1).astype(BF16)
    pad = jnp.zeros((D_MODEL, LANES - 3 * SSM_HEADS), F32)
    w_dt = jnp.concatenate([wdt, wdt, wdt, pad], axis=1).astype(BF16)

    proj, dt_raw = _in_proj(hf, p["norm1_w"].reshape(1, D_MODEL), w_main, w_dt)

    qk_scale = ATTN_HEAD_DIM ** -0.5 * LOG2E
    qg = (jnp.tile(p["q_norm_w"], 2) * qk_scale).reshape(1, LANES)
    kg = jnp.tile(p["k_norm_w"], 2).reshape(1, LANES)
    slopes = jnp.power(2.0, -8.0 * (jnp.arange(ATTN_HEADS, dtype=F32) + 1.0) / ATTN_HEADS) * LOG2E
    c_tab = jnp.broadcast_to(slopes[:, None, None], (ATTN_HEADS, 1, LANES))
    c_row_tab = jnp.broadcast_to(slopes[:, None, None], (ATTN_HEADS, 1, ATTN_TILE))
    lam = (jnp.exp(jnp.sum(p["lambda_q1"] * p["lambda_k1"]))
           - jnp.exp(jnp.sum(p["lambda_q2"] * p["lambda_k2"])) + lambda_init).reshape(1)
    qT, ka, vT = _attn_prep(proj, qg, kg, c_tab, B, S)
    attn = _attention(lam, qT, ka, vT, c_row_tab, p["subln_w"].reshape(LANES, 1), B, S, lambda_init)

    rep3 = lambda v: jnp.concatenate([v, v, v, jnp.zeros((LANES - 3 * SSM_HEADS,), F32)]).reshape(1, LANES)
    head_of_row = jnp.arange(LANES) % SSM_HEADS
    head_of_col = jnp.arange(D_INNER) // SSM_HEAD_DIM
    expand = ((head_of_row[:, None] == head_of_col[None, :])
              & (jnp.arange(LANES)[:, None] < 3 * SSM_HEADS)).astype(BF16)
    ssm = _ssd(proj, dt_raw, p["conv_w"], p["conv_b"].reshape(1, CONV_DIM), rep3(p["dt_bias"]),
               rep3(p["a_log"]), jnp.repeat(p["d_skip"], SSM_HEAD_DIM).reshape(1, D_INNER),
               p["ssm_norm_w"].reshape(1, D_INNER), expand, B, S)

    wr = jnp.pad(p["w_router"], ((0, 0), (0, LANES - N_EXPERTS))).astype(BF16)
    br = jnp.pad(p["b_router"], (0, LANES - N_EXPERTS)).reshape(1, LANES)
    h1, hn, logits = _merge(attn, ssm, proj, hf, p["w_attn_proj"].astype(BF16),
                            p["w_ssm_proj"].astype(BF16), p["w_out"].astype(BF16),
                            p["norm2_w"].reshape(1, D_MODEL), wr, br)

    idx, w_top, lpos = _route(logits)
    tr = min(ROUTE_TILE, T)
    experts = jnp.arange(N_EXPERTS, dtype=I32)
    chosen = idx.reshape(TOP_K, T // tr, tr)[..., None] == experts
    tile_cnt = jnp.sum(chosen, axis=(0, 2), dtype=I32)
    tile_cnt = (tile_cnt + ROW_ALIGN - 1) // ROW_ALIGN * ROW_ALIGN
    counts = jnp.sum(tile_cnt, axis=0)
    tiles_per = (counts + MOE_TILE - 1) // MOE_TILE
    tile_end = jnp.cumsum(tiles_per)
    group_start = (tile_end - tiles_per) * MOE_TILE
    n_runs = (T // tr) * N_EXPERTS
    n_tiles = -(-(T * TOP_K + n_runs * (ROW_ALIGN - 1)) // MOE_TILE) + N_EXPERTS
    M_pad = n_tiles * MOE_TILE
    tile_expert = jnp.minimum(jnp.sum(jnp.arange(n_tiles)[:, None] >= tile_end[None, :], axis=1),
                              N_EXPERTS - 1).astype(I32)
    n_used = tile_end[-1:].astype(I32)
    run_row = (group_start[None, :] + jnp.cumsum(tile_cnt, axis=0) - tile_cnt).reshape(-1)
    run_cnt = tile_cnt.reshape(-1)

    xs = _dispatch(run_row, run_cnt, group_start + counts, tiles_per * MOE_TILE - counts,
                   lpos, hn, M_pad)
    ys = _ffn(tile_expert, n_used, xs, p["w_gate_up"].astype(BF16),
              p["b_gate_up"].reshape(N_EXPERTS, 1, 2 * D_FF), p["w_down"].astype(BF16),
              p["b_down"].reshape(N_EXPERTS, 1, D_MODEL))
    out = _combine(run_row, run_cnt, lpos.T, w_top.T, h1, ys)
    return out.reshape(B, S, D_MODEL)


def kernel(x, norm1_w, w_in, q_norm_w, k_norm_w, lambda_q1, lambda_k1, lambda_q2, lambda_k2, subln_w, conv_w, conv_b, dt_bias, a_log, d_skip, ssm_norm_w, w_attn_proj, w_ssm_proj, w_out, norm2_w, w_router, b_router, w_gate_up, b_gate_up, w_down, b_down):
    params = dict(norm1_w=norm1_w, w_in=w_in, q_norm_w=q_norm_w, k_norm_w=k_norm_w,
                  lambda_q1=lambda_q1, lambda_k1=lambda_k1, lambda_q2=lambda_q2, lambda_k2=lambda_k2,
                  subln_w=subln_w, conv_w=conv_w, conv_b=conv_b, dt_bias=dt_bias, a_log=a_log,
                  d_skip=d_skip, ssm_norm_w=ssm_norm_w, w_attn_proj=w_attn_proj,
                  w_ssm_proj=w_ssm_proj, w_out=w_out, norm2_w=norm2_w, w_router=w_router,
                  b_router=b_router, w_gate_up=w_gate_up, b_gate_up=b_gate_up, w_down=w_down,
                  b_down=b_down)
    h = x
    for l in range(w_in.shape[0]):
        h = _layer(h, l, {k: v[l] for k, v in params.items()})
    return h
```

```python
import functools
import math

import jax
import jax.numpy as jnp
from jax import lax
from jax.experimental import pallas as pl
from jax.experimental.pallas import tpu as pltpu

F32, BF16, I32, U32 = jnp.float32, jnp.bfloat16, jnp.int32, jnp.uint32

D_MODEL = 1024
ATTN_HEADS = 8
ATTN_HEAD_DIM = 64
ATTN_WIDTH = ATTN_HEADS * 2 * ATTN_HEAD_DIM
SSM_EXPAND = 2
D_INNER = SSM_EXPAND * D_MODEL
SSM_HEAD_DIM = 64
SSM_HEADS = D_INNER // SSM_HEAD_DIM
SSM_GROUPS = 4
SSM_STATE = 128
CONV_WIDTH = 4
CONV_DIM = D_INNER + 2 * SSM_GROUPS * SSM_STATE
SSM_CHUNK = 128
N_EXPERTS = 32
TOP_K = 4
D_FF = D_MODEL
SWIGLU_LIMIT = 7.0
SWIGLU_ALPHA = 1.702
EPS = 1e-5
QK_EPS = 1e-6

LANES = 128
HEADS_PER_GROUP = SSM_HEADS // SSM_GROUPS
GROUP_WIDTH = D_INNER // SSM_GROUPS
LOG2E = math.log2(math.e)
NEG = -1e30
VMEM_LIMIT = 56 * 1024 * 1024

COL_Q, COL_K, COL_V = 0, ATTN_WIDTH, 2 * ATTN_WIDTH
COL_XBC = 3 * ATTN_WIDTH
COL_Z = COL_XBC + CONV_DIM
COL_GATE = COL_Z + D_INNER
PROJ_W = COL_GATE + 2 * D_MODEL

ATTN_TILE = 512
ATTN_HEADS_PER_STEP = 2
VT_ROWS = 2 * ATTN_HEAD_DIM + 16
MOE_TILE = 512
ROUTE_TILE = 256
CONV_HALO = 16
PACKED_W = D_MODEL // 2
ROW_ALIGN = 8
ROW_CHUNKS = (8, 16, 32, 64, 128, 256)


def _sort_rows(tm):
    return TOP_K * tm + N_EXPERTS * ROW_ALIGN


def _params(*sem):
    return pltpu.CompilerParams(dimension_semantics=sem, vmem_limit_bytes=VMEM_LIMIT)


def _sigmoid(x):
    return 0.5 * jnp.tanh(0.5 * x) + 0.5


def _silu(x):
    h = 0.5 * x
    return h + h * jnp.tanh(h)


def _pack_bf16_pairs(x):
    w = x.shape[1] // 2
    lo = lax.bitcast_convert_type(x[:, :w], U32) >> 16
    hi = lax.bitcast_convert_type(x[:, w:], U32) & jnp.uint32(0xFFFF0000)
    return hi | lo


def _unpack_bf16_pairs(p):
    lo = lax.bitcast_convert_type(p << 16, F32)
    hi = lax.bitcast_convert_type(p & jnp.uint32(0xFFFF0000), F32)
    return jnp.concatenate([lo, hi], axis=1).astype(BF16)


def _inproj_body(x_ref, g_ref, w_ref, wdt_ref, o_ref, dt_ref, u_scr):
    @pl.when(pl.program_id(1) == 0)
    def _():
        x = x_ref[...]
        u = x * lax.rsqrt(jnp.mean(x * x, axis=-1, keepdims=True) + EPS) * g_ref[...]
        ub = u.astype(BF16)
        u_scr[...] = ub
        dt_ref[...] = jnp.dot(ub, wdt_ref[...], preferred_element_type=F32)

    o_ref[...] = jnp.dot(u_scr[...], w_ref[...], preferred_element_type=F32).astype(o_ref.dtype)


def _in_proj(xf, gain, w_main, w_dt):
    T = xf.shape[0]
    tm = min(1024, T)
    tn = 2048
    return pl.pallas_call(
        _inproj_body,
        grid=(T // tm, PROJ_W // tn),
        in_specs=[
            pl.BlockSpec((tm, D_MODEL), lambda i, j: (i, 0)),
            pl.BlockSpec((1, D_MODEL), lambda i, j: (0, 0)),
            pl.BlockSpec((D_MODEL, tn), lambda i, j: (0, j)),
            pl.BlockSpec((D_MODEL, LANES), lambda i, j: (0, 0)),
        ],
        out_specs=[
            pl.BlockSpec((tm, tn), lambda i, j: (i, j)),
            pl.BlockSpec((tm, LANES), lambda i, j: (i, 0)),
        ],
        out_shape=[
            jax.ShapeDtypeStruct((T, PROJ_W), BF16),
            jax.ShapeDtypeStruct((T, LANES), F32),
        ],
        scratch_shapes=[pltpu.VMEM((tm, D_MODEL), BF16)],
        compiler_params=_params("parallel", "arbitrary"),
        name="in_proj",
    )(xf, gain, w_main, w_dt)


def _split3(x):
    hi = x.astype(BF16).astype(F32)
    r = x - hi
    mid = r.astype(BF16).astype(F32)
    return hi, mid, r - mid


def _attn_prep_body(q_ref, k_ref, v_ref, qg_ref, kg_ref, c_ref, qT_ref, ka_ref, vT_ref, *, tk):
    ts = q_ref.shape[0]
    lane = lax.broadcasted_iota(I32, (ts, LANES), 1)
    lo_half = lane < ATTN_HEAD_DIM

    def half_norm(x, g):
        x2 = x * x
        s_lo = jnp.sum(jnp.where(lo_half, x2, 0.0), axis=-1, keepdims=True)
        s_hi = jnp.sum(jnp.where(lo_half, 0.0, x2), axis=-1, keepdims=True)
        ms = jnp.where(lo_half, s_lo, s_hi) * (1.0 / ATTN_HEAD_DIM)
        return x * lax.rsqrt(ms + QK_EPS) * g

    q = half_norm(q_ref[...].astype(F32), qg_ref[...])
    k = half_norm(k_ref[...].astype(F32), kg_ref[...])

    row = lax.broadcasted_iota(I32, (ts, LANES), 0) + pl.program_id(2) * ts
    bias = c_ref[...] * (row % tk).astype(F32)
    b_hi, b_mid, b_lo = _split3(bias)
    k_tail = jnp.where(lane == ATTN_HEAD_DIM, b_hi,
                       jnp.where(lane == ATTN_HEAD_DIM + 1, b_mid,
                                 jnp.where(lane == ATTN_HEAD_DIM + 2, b_lo, 0.0)))
    q_tail = jnp.where(lane < ATTN_HEAD_DIM + 3, 1.0, 0.0)

    for comp in range(2):
        qc = q if comp == 0 else pltpu.roll(q, ATTN_HEAD_DIM, 1)
        kc = k if comp == 0 else pltpu.roll(k, ATTN_HEAD_DIM, 1)
        qT_ref[comp] = jnp.where(lo_half, qc, q_tail).T.astype(BF16)
        ka_ref[comp] = jnp.where(lo_half, kc, k_tail).astype(BF16)
    dv = 2 * ATTN_HEAD_DIM
    vT_ref[0:dv, :] = v_ref[...].astype(F32).T.astype(BF16)
    vT_ref[dv:VT_ROWS, :] = jnp.ones((VT_ROWS - dv, ts), BF16)


def _attn_prep(proj, qg, kg, c_tab, B, S):
    ts = min(1024, S)
    ns = S // ts
    H = ATTN_HEADS
    hb = ATTN_WIDTH // LANES
    return pl.pallas_call(
        functools.partial(_attn_prep_body, tk=ATTN_TILE),
        grid=(B, H, ns),
        in_specs=[
            pl.BlockSpec((ts, LANES), lambda b, h, s: (b * ns + s, h)),
            pl.BlockSpec((ts, LANES), lambda b, h, s: (b * ns + s, hb + h)),
            pl.BlockSpec((ts, LANES), lambda b, h, s: (b * ns + s, 2 * hb + h)),
            pl.BlockSpec((1, LANES), lambda b, h, s: (0, 0)),
            pl.BlockSpec((1, LANES), lambda b, h, s: (0, 0)),
            pl.BlockSpec((None, 1, LANES), lambda b, h, s: (h, 0, 0)),
        ],
        out_specs=[
            pl.BlockSpec((None, None, 2, LANES, ts), lambda b, h, s: (b, h, 0, 0, s)),
            pl.BlockSpec((None, None, 2, ts, LANES), lambda b, h, s: (b, h, 0, s, 0)),
            pl.BlockSpec((None, None, VT_ROWS, ts), lambda b, h, s: (b, h, 0, s)),
        ],
        out_shape=[
            jax.ShapeDtypeStruct((B, H, 2, LANES, S), BF16),
            jax.ShapeDtypeStruct((B, H, 2, S, LANES), BF16),
            jax.ShapeDtypeStruct((B, H, VT_ROWS, S), BF16),
        ],
        compiler_params=_params("parallel", "parallel", "parallel"),
        name="attn_prep",
    )(proj, proj, proj, qg, kg, c_tab)


def _attn_body(lam_ref, qT_ref, ka_ref, vT_ref, c_ref, w_ref, o_ref, *scratch, tile, out_scale):
    i = pl.program_id(2)
    streams = [(h, comp) for h in range(ATTN_HEADS_PER_STEP) for comp in range(2)]
    ns = len(streams)
    slot_a, slot_b, accs = scratch[:ns], scratch[ns:2 * ns], scratch[2 * ns:]
    for acc in accs:
        acc[...] = jnp.zeros_like(acc)
    c_rows = [c_ref[h] for h in range(ATTN_HEADS_PER_STEP)]

    def scores(j, slot):
        start = pl.multiple_of(j * tile, tile)
        for n, (h, comp) in enumerate(streams):
            slot[n][...] = jnp.dot(ka_ref[h, comp, pl.ds(start, tile), :], qT_ref[h, comp],
                                   preferred_element_type=F32)

    def softmax_pv(j, slot, ms, masked):
        start = pl.multiple_of(j * tile, tile)
        new_ms = []
        for n, (h, comp) in enumerate(streams):
            cb = c_rows[h] * ((j - i) * tile).astype(F32)
            sT = slot[n][...]
            if masked:
                key = lax.broadcasted_iota(I32, sT.shape, 0)
                qry = lax.broadcasted_iota(I32, sT.shape, 1)
                sT = jnp.where(key <= qry, sT, NEG)
            m_new = jnp.maximum(ms[n], jnp.max(sT, axis=0, keepdims=True) + cb)
            alpha = jnp.exp2(ms[n] - m_new)
            p = jnp.exp2(sT - (m_new - cb)).astype(BF16)
            pv = jnp.dot(vT_ref[h, :, pl.ds(start, tile)], p, preferred_element_type=F32)
            accs[n][...] = alpha * accs[n][...] + pv
            new_ms.append(m_new)
        return tuple(new_ms)

    def finish(slot, ms):
        softmax_pv(i, slot, ms, True)
        dv = 2 * ATTN_HEAD_DIM
        for h in range(ATTN_HEADS_PER_STEP):
            outs = [accs[2 * h + comp][0:dv, :] / accs[2 * h + comp][dv:dv + 1, :]
                    for comp in range(2)]
            o = outs[0] - lam_ref[0] * outs[1]
            msq = jnp.mean(o * o, axis=0, keepdims=True)
            y = o * lax.rsqrt(msq + EPS) * w_ref[...] * out_scale
            o_ref[:, h * dv:(h + 1) * dv] = y.T.astype(o_ref.dtype)

    def body(t, ms):
        j = 2 * t
        scores(j + 1, slot_b)
        ms = softmax_pv(j, slot_a, ms, False)
        scores(j + 2, slot_a)
        return softmax_pv(j + 1, slot_b, ms, False)

    scores(0, slot_a)
    m_init = jnp.full((1, tile), NEG, F32)
    ms = lax.fori_loop(0, i // 2, body, (m_init,) * ns)

    @pl.when(i % 2 == 0)
    def _():
        finish(slot_a, ms)

    @pl.when(i % 2 == 1)
    def _():
        scores(i, slot_b)
        finish(slot_b, softmax_pv(i - 1, slot_a, ms, False))


def _attention(lam, qT, ka, vT, c_row_tab, subln_col, B, S, lambda_init):
    tile = ATTN_TILE
    nh = ATTN_HEADS_PER_STEP
    nq = S // tile
    n_streams = 2 * nh
    return pl.pallas_call(
        functools.partial(_attn_body, tile=tile, out_scale=1.0 - lambda_init),
        grid=(B, ATTN_HEADS // nh, nq),
        in_specs=[
            pl.BlockSpec(memory_space=pltpu.SMEM),
            pl.BlockSpec((None, nh, 2, LANES, tile), lambda b, g, i: (b, g, 0, 0, i)),
            pl.BlockSpec((None, nh, 2, S, LANES), lambda b, g, i: (b, g, 0, 0, 0)),
            pl.BlockSpec((None, nh, VT_ROWS, S), lambda b, g, i: (b, g, 0, 0)),
            pl.BlockSpec((nh, 1, tile), lambda b, g, i: (g, 0, 0)),
            pl.BlockSpec((LANES, 1), lambda b, g, i: (0, 0)),
        ],
        out_specs=pl.BlockSpec((tile, nh * LANES), lambda b, g, i: (b * nq + i, g)),
        out_shape=jax.ShapeDtypeStruct((B * S, ATTN_WIDTH), BF16),
        scratch_shapes=([pltpu.VMEM((tile, tile), F32)] * (2 * n_streams)
                        + [pltpu.VMEM((VT_ROWS, tile), F32)] * n_streams),
        compiler_params=_params("parallel", "parallel", "arbitrary"),
        name="diff_attn",
    )(lam, qT, ka, vT, c_row_tab, subln_col)


def _expand_heads(x, e_ref):
    lane = lax.broadcasted_iota(I32, x.shape, 1)
    hi, mid, lo = _split3(x)
    parts = jnp.where(lane < SSM_HEADS, hi,
                      jnp.where(lane < 2 * SSM_HEADS, mid,
                                jnp.where(lane < 3 * SSM_HEADS, lo, 0.0)))
    return jnp.dot(parts.astype(BF16), e_ref[...], preferred_element_type=F32)


def _ssd_body(xbc_ref, z_ref, dt_ref, cw_ref, cb_ref, dtb_ref, alog_ref, dskip_ref, nw_ref, e_ref,
              o_ref, ext_ref, state_ref, y_ref):
    Q = SSM_CHUNK
    halo = CONV_HALO

    @pl.when(pl.program_id(1) == 0)
    def _():
        ext_ref[0:halo, :] = jnp.zeros((halo, CONV_DIM), BF16)
        state_ref[...] = jnp.zeros_like(state_ref)

    xb = xbc_ref[...]
    ext_ref[halo:halo + Q, :] = xb
    taps = CONV_WIDTH - 1
    out_row = lax.broadcasted_iota(I32, (taps * Q, halo + Q), 0)
    src_row = lax.broadcasted_iota(I32, (taps * Q, halo + Q), 1)
    shift = (src_row == out_row - (out_row // Q) * (Q - 1) + (halo - taps)).astype(BF16)
    shifted = jnp.dot(shift, ext_ref[...], preferred_element_type=F32)
    conv = cb_ref[...] + cw_ref[taps:taps + 1, :] * xb.astype(F32)
    for w in range(taps):
        conv = conv + cw_ref[w:w + 1, :] * shifted[w * Q:(w + 1) * Q, :]
    ext_ref[0:halo, :] = xb[Q - halo:Q, :]
    xc = _silu(conv)
    xs = xc[:, :D_INNER]

    raw = dt_ref[...] + dtb_ref[...]
    dt = jnp.maximum(raw, 0.0) + jnp.log2(1.0 + jnp.exp(-jnp.abs(raw))) * math.log(2.0)
    dA = dt * (-jnp.exp(alog_ref[...]))
    t_idx = lax.broadcasted_iota(I32, (Q, Q), 0)
    s_idx = lax.broadcasted_iota(I32, (Q, Q), 1)
    causal = s_idx <= t_idx
    cum = jnp.dot(causal.astype(F32), dA, preferred_element_type=F32,
                  precision=lax.Precision.HIGHEST)
    cumT = cum.T
    ecum = jnp.exp(cum)
    decay = jnp.exp(cum[Q - 1:Q, :] - cum)

    dt_e = _expand_heads(dt, e_ref)
    ecum_e = _expand_heads(ecum, e_ref)
    decay_e = _expand_heads(decay, e_ref)
    xdt = xs * dt_e
    xdt_b = xdt.astype(BF16)
    xdec_b = (xdt * decay_e).astype(BF16)

    lane = lax.broadcasted_iota(I32, (Q, LANES), 1)
    first_head = lane < SSM_HEAD_DIM
    for g in range(SSM_GROUPS):
        bcol = D_INNER + g * SSM_STATE
        ccol = D_INNER + SSM_GROUPS * SSM_STATE + g * SSM_STATE
        Bg = xc[:, bcol:bcol + SSM_STATE].astype(BF16)
        Cg = xc[:, ccol:ccol + SSM_STATE].astype(BF16)
        gs = slice(g * GROUP_WIDTH, (g + 1) * GROUP_WIDTH)
        CB = lax.dot_general(Cg, Bg, (((1,), (1,)), ((), ())), preferred_element_type=F32)
        state = state_ref[g]
        y_off = jnp.dot(Cg, state.astype(BF16), preferred_element_type=F32) * ecum_e[:, gs]
        for pair in range(HEADS_PER_GROUP // 2):
            h0 = g * HEADS_PER_GROUP + 2 * pair
            ms = []
            for h in (h0, h0 + 1):
                diff = cum[:, h:h + 1] - cumT[h:h + 1, :]
                L = jnp.exp(jnp.where(causal, diff, NEG))
                ms.append((CB * L).astype(BF16))
            lhs = jnp.concatenate(ms, axis=1)
            c0 = h0 * SSM_HEAD_DIM
            xp = xdt_b[:, c0:c0 + LANES]
            zero = jnp.zeros_like(xp)
            rhs = jnp.concatenate([jnp.where(first_head, xp, zero),
                                   jnp.where(first_head, zero, xp)], axis=0)
            y_ref[:, c0:c0 + LANES] = (jnp.dot(lhs, rhs, preferred_element_type=F32)
                                       + y_off[:, c0 - g * GROUP_WIDTH:c0 - g * GROUP_WIDTH + LANES])
        upd = lax.dot_general(Bg, xdec_b[:, gs], (((0,), (0,)), ((), ())),
                              preferred_element_type=F32)
        state_ref[g] = state * ecum_e[Q - 1:Q, gs] + upd

    y = y_ref[...] + xs * dskip_ref[...]
    z = z_ref[...].astype(F32)
    y = y * _silu(z)
    for g in range(SSM_GROUPS):
        gs = slice(g * GROUP_WIDTH, (g + 1) * GROUP_WIDTH)
        yg = y[:, gs]
        ms = jnp.mean(yg * yg, axis=-1, keepdims=True)
        o_ref[:, gs] = (yg * lax.rsqrt(ms + EPS) * nw_ref[:, gs]).astype(o_ref.dtype)


def _ssd(proj, dt_raw, conv_w, conv_b, dtb3, alog3, dskip_e, norm_w, expand, B, S):
    Q = SSM_CHUNK
    nc = S // Q
    row = lambda b, c: b * nc + c
    const = lambda b, c: (0, 0)
    return pl.pallas_call(
        _ssd_body,
        grid=(B, nc),
        in_specs=[
            pl.BlockSpec((Q, CONV_DIM), lambda b, c: (row(b, c), COL_XBC // CONV_DIM)),
            pl.BlockSpec((Q, D_INNER), lambda b, c: (row(b, c), COL_Z // D_INNER)),
            pl.BlockSpec((Q, LANES), lambda b, c: (row(b, c), 0)),
            pl.BlockSpec((CONV_WIDTH, CONV_DIM), const),
            pl.BlockSpec((1, CONV_DIM), const),
            pl.BlockSpec((1, LANES), const),
            pl.BlockSpec((1, LANES), const),
            pl.BlockSpec((1, D_INNER), const),
            pl.BlockSpec((1, D_INNER), const),
            pl.BlockSpec((LANES, D_INNER), const),
        ],
        out_specs=pl.BlockSpec((Q, D_INNER), lambda b, c: (row(b, c), 0)),
        out_shape=jax.ShapeDtypeStruct((B * S, D_INNER), BF16),
        scratch_shapes=[
            pltpu.VMEM((Q + CONV_HALO, CONV_DIM), BF16),
            pltpu.VMEM((SSM_GROUPS, SSM_STATE, GROUP_WIDTH), F32),
            pltpu.VMEM((Q, D_INNER), F32),
        ],
        compiler_params=_params("parallel", "arbitrary"),
        name="ssd",
    )(proj, proj, dt_raw, conv_w, conv_b, dtb3, alog3, dskip_e, norm_w, expand)


def _merge_body(attn_ref, ssm_ref, gate_ref, x_ref, wa_ref, ws_ref, wo_ref, n2_ref, wr_ref, br_ref,
                h1_ref, hn_ref, lg_ref):
    gate = gate_ref[...].astype(F32)
    ya = jnp.dot(attn_ref[...], wa_ref[...], preferred_element_type=F32)
    ys = jnp.dot(ssm_ref[...], ws_ref[...], preferred_element_type=F32)
    mixed = _sigmoid(gate[:, :D_MODEL]) * ya + _sigmoid(gate[:, D_MODEL:]) * ys
    h1 = x_ref[...] + jnp.dot(mixed.astype(BF16), wo_ref[...], preferred_element_type=F32)
    h1_ref[...] = h1
    hn = h1 * lax.rsqrt(jnp.mean(h1 * h1, axis=-1, keepdims=True) + EPS) * n2_ref[...]
    hn_ref[...] = hn
    lg_ref[...] = jnp.dot(hn.astype(BF16), wr_ref[...], preferred_element_type=F32) + br_ref[...]


def _merge(attn, ssm, proj, xf, wa, ws, wo, n2, wr, br):
    T = xf.shape[0]
    tm = min(256, T)
    const = lambda i: (0, 0)
    return pl.pallas_call(
        _merge_body,
        grid=(T // tm,),
        in_specs=[
            pl.BlockSpec((tm, ATTN_WIDTH), lambda i: (i, 0)),
            pl.BlockSpec((tm, D_INNER), lambda i: (i, 0)),
            pl.BlockSpec((tm, 2 * D_MODEL), lambda i: (i, COL_GATE // (2 * D_MODEL))),
            pl.BlockSpec((tm, D_MODEL), lambda i: (i, 0)),
            pl.BlockSpec((ATTN_WIDTH, D_MODEL), const),
            pl.BlockSpec((D_INNER, D_MODEL), const),
            pl.BlockSpec((D_MODEL, D_MODEL), const),
            pl.BlockSpec((1, D_MODEL), const),
            pl.BlockSpec((D_MODEL, LANES), const),
            pl.BlockSpec((1, LANES), const),
        ],
        out_specs=[
            pl.BlockSpec((tm, D_MODEL), lambda i: (i, 0)),
            pl.BlockSpec((tm, D_MODEL), lambda i: (i, 0)),
            pl.BlockSpec((tm, LANES), lambda i: (i, 0)),
        ],
        out_shape=[
            jax.ShapeDtypeStruct((T, D_MODEL), F32),
            jax.ShapeDtypeStruct((T, D_MODEL), F32),
            jax.ShapeDtypeStruct((T, LANES), F32),
        ],
        compiler_params=_params("parallel"),
        name="merge",
    )(attn, ssm, proj, xf, wa, ws, wo, n2, wr, br)


def _route_body(lg_ref, idx_ref, w_ref, lpos_ref):
    tm = lg_ref.shape[0]
    logit = lg_ref[...].T[:N_EXPERTS, :]
    eio = lax.broadcasted_iota(I32, logit.shape, 0)
    vals, hits = [], []
    for k in range(TOP_K):
        mx = jnp.max(logit, axis=0, keepdims=True)
        idx = jnp.min(jnp.where(logit == mx, eio, N_EXPERTS), axis=0, keepdims=True)
        hit = eio == idx
        logit = jnp.where(hit, -jnp.inf, logit)
        idx_ref[k:k + 1, :] = idx
        vals.append(mx)
        hits.append(hit)
    exps = [jnp.exp(v - vals[0]) for v in vals]
    denom = exps[0] + exps[1] + exps[2] + exps[3]
    for k in range(TOP_K):
        w_ref[k:k + 1, :] = exps[k] / denom

    sel = (hits[0] | hits[1] | hits[2] | hits[3]).astype(BF16)
    before = (lax.broadcasted_iota(I32, (tm, tm), 0) < lax.broadcasted_iota(I32, (tm, tm), 1))
    prefix = jnp.dot(sel, before.astype(BF16), preferred_element_type=F32)
    count = jnp.sum(sel.astype(F32), axis=1, keepdims=True)
    count = jnp.floor((count + (ROW_ALIGN - 1)) * (1.0 / ROW_ALIGN)) * ROW_ALIGN
    count = jnp.broadcast_to(count, (N_EXPERTS, LANES))
    lower = (lax.broadcasted_iota(I32, (N_EXPERTS, N_EXPERTS), 1)
             < lax.broadcasted_iota(I32, (N_EXPERTS, N_EXPERTS), 0))
    start = jnp.dot(lower.astype(BF16), count.astype(BF16), preferred_element_type=F32)[:, 0:1]
    offs = prefix + start
    for k in range(TOP_K):
        lpos_ref[k:k + 1, :] = jnp.sum(jnp.where(hits[k], offs, 0.0), axis=0,
                                       keepdims=True).astype(I32)


def _route(logits):
    T = logits.shape[0]
    tm = min(ROUTE_TILE, T)
    spec = pl.BlockSpec((TOP_K, tm), lambda i: (0, i))
    return pl.pallas_call(
        _route_body,
        grid=(T // tm,),
        in_specs=[pl.BlockSpec((tm, LANES), lambda i: (i, 0))],
        out_specs=[spec, spec, spec],
        out_shape=[
            jax.ShapeDtypeStruct((TOP_K, T), I32),
            jax.ShapeDtypeStruct((TOP_K, T), F32),
            jax.ShapeDtypeStruct((TOP_K, T), I32),
        ],
        compiler_params=_params("parallel"),
        name="route",
    )(logits)


def _for_row_chunks(count, fn):
    for size in ROW_CHUNKS:
        @pl.when((count & size) != 0)
        def _():
            fn(count & (size - 1), size)


def _rows_copy(src_ref, src_row, dst_ref, dst_row, size, sem):
    return pltpu.make_async_copy(src_ref.at[pl.ds(pl.multiple_of(src_row, ROW_ALIGN), size), :],
                                 dst_ref.at[pl.ds(pl.multiple_of(dst_row, ROW_ALIGN), size), :], sem)


def _dispatch_body(row_ref, cnt_ref, pad_row_ref, pad_cnt_ref, lpos_ref, hn_ref, xs_ref,
                   sorted_ref, zero_ref, sems, pad_sem):
    t = pl.program_id(0)
    nt = pl.num_programs(0)
    tm = hn_ref.shape[0]
    slot = t % 2

    @pl.when(t == 0)
    def _():
        zero_ref[...] = jnp.zeros_like(zero_ref)
        for wait in (False, True):
            def pad(e, carry):
                def one(off, size):
                    cp = _rows_copy(zero_ref, 0, xs_ref, pad_row_ref[e] + off, size, pad_sem)
                    cp.wait() if wait else cp.start()
                _for_row_chunks(pad_cnt_ref[e], one)
                return carry
            lax.fori_loop(0, N_EXPERTS, pad, 0)

    lp = lpos_ref[...]
    r = lax.broadcasted_iota(I32, (_sort_rows(tm), tm), 0)
    onehot = jnp.where(r == lp[0:1], 1.0, jnp.where(r == lp[1:2], 1.0,
             jnp.where(r == lp[2:3], 1.0, jnp.where(r == lp[3:4], 1.0, 0.0))))
    sorted_ref[slot] = _pack_bf16_pairs(jnp.dot(onehot.astype(BF16), hn_ref[...].astype(BF16),
                                                preferred_element_type=F32))

    def runs(tile, buf_slot, wait):
        def one_expert(e, off):
            n = cnt_ref[tile * N_EXPERTS + e]
            dst = row_ref[tile * N_EXPERTS + e]
            def one(sub, size):
                cp = _rows_copy(sorted_ref.at[buf_slot], off + sub, xs_ref, dst + sub, size,
                                sems.at[buf_slot])
                cp.wait() if wait else cp.start()
            _for_row_chunks(n, one)
            return off + n
        lax.fori_loop(0, N_EXPERTS, one_expert, 0)

    runs(t, slot, False)

    @pl.when(t > 0)
    def _():
        runs(t - 1, 1 - slot, True)

    @pl.when(t == nt - 1)
    def _():
        runs(t, slot, True)


def _dispatch(run_row, run_cnt, pad_row, pad_cnt, lpos, hn, m_pad):
    T = hn.shape[0]
    tm = min(ROUTE_TILE, T)
    return pl.pallas_call(
        _dispatch_body,
        grid_spec=pltpu.PrefetchScalarGridSpec(
            num_scalar_prefetch=4,
            grid=(T // tm,),
            in_specs=[
                pl.BlockSpec((TOP_K, tm), lambda i, *_: (0, i)),
                pl.BlockSpec((tm, D_MODEL), lambda i, *_: (i, 0)),
            ],
            out_specs=pl.BlockSpec(memory_space=pl.ANY),
            scratch_shapes=[
                pltpu.VMEM((2, _sort_rows(tm), PACKED_W), U32),
                pltpu.VMEM((ROW_CHUNKS[-1], PACKED_W), U32),
                pltpu.SemaphoreType.DMA((2,)),
                pltpu.SemaphoreType.DMA(()),
            ],
        ),
        out_shape=jax.ShapeDtypeStruct((m_pad, PACKED_W), U32),
        compiler_params=_params("arbitrary"),
        name="dispatch",
    )(run_row, run_cnt, pad_row, pad_cnt, lpos, hn)


def _ffn_body(te_ref, nu_ref, xs_ref, wgu_ref, bgu_ref, wd_ref, bd_ref, ys_ref, wgu_bf, wd_bf):
    i = pl.program_id(0)

    @pl.when((i == 0) | (te_ref[i] != te_ref[jnp.maximum(i, 1) - 1]))
    def _():
        wgu_bf[...] = wgu_ref[...].astype(BF16)
        wd_bf[...] = wd_ref[...].astype(BF16)

    @pl.when(i < nu_ref[0])
    def _():
        gu = jnp.dot(_unpack_bf16_pairs(xs_ref[...]), wgu_bf[...],
                     preferred_element_type=F32) + bgu_ref[...]
        glu = jnp.minimum(gu[:, :D_FF], SWIGLU_LIMIT)
        lin = jnp.clip(gu[:, D_FF:], -SWIGLU_LIMIT, SWIGLU_LIMIT)
        act = glu * _sigmoid(SWIGLU_ALPHA * glu) * (lin + 1.0)
        y = jnp.dot(act.astype(BF16), wd_bf[...], preferred_element_type=F32) + bd_ref[...]
        ys_ref[...] = _pack_bf16_pairs(y.astype(BF16).astype(F32))


def _ffn(tile_expert, n_used, xs, wgu, bgu, wd, bd):
    M_pad = xs.shape[0]
    tm = MOE_TILE
    n_tiles = M_pad // tm
    row = lambda i, te, nu: (jnp.minimum(i, nu[0] - 1), 0)
    return pl.pallas_call(
        _ffn_body,
        grid_spec=pltpu.PrefetchScalarGridSpec(
            num_scalar_prefetch=2,
            grid=(n_tiles,),
            in_specs=[
                pl.BlockSpec((tm, PACKED_W), row),
                pl.BlockSpec((None, D_MODEL, 2 * D_FF), lambda i, te, nu: (te[i], 0, 0)),
                pl.BlockSpec((None, 1, 2 * D_FF), lambda i, te, nu: (te[i], 0, 0)),
                pl.BlockSpec((None, D_FF, D_MODEL), lambda i, te, nu: (te[i], 0, 0)),
                pl.BlockSpec((None, 1, D_MODEL), lambda i, te, nu: (te[i], 0, 0)),
            ],
            out_specs=pl.BlockSpec((tm, PACKED_W), row),
            scratch_shapes=[pltpu.VMEM((D_MODEL, 2 * D_FF), BF16), pltpu.VMEM((D_FF, D_MODEL), BF16)],
        ),
        out_shape=jax.ShapeDtypeStruct((M_pad, PACKED_W), U32),
        compiler_params=_params("arbitrary"),
        name="expert_ffn",
    )(tile_expert, n_used, xs, wgu, bgu, wd, bd)


def _combine_body(row_ref, cnt_ref, lpos_ref, w_ref, h1_ref, ys_ref, o_ref, buf_ref, sems):
    t = pl.program_id(0)
    nt = pl.num_programs(0)
    tm = h1_ref.shape[0]
    slot = t % 2

    def runs(tile, buf_slot, wait):
        def one_expert(e, off):
            n = cnt_ref[tile * N_EXPERTS + e]
            src = row_ref[tile * N_EXPERTS + e]
            def one(sub, size):
                cp = _rows_copy(ys_ref, src + sub, buf_ref.at[buf_slot], off + sub, size,
                                sems.at[buf_slot])
                cp.wait() if wait else cp.start()
            _for_row_chunks(n, one)
            return off + n
        lax.fori_loop(0, N_EXPERTS, one_expert, 0)

    @pl.when(t == 0)
    def _():
        buf_ref[...] = jnp.zeros_like(buf_ref)
        runs(t, slot, False)

    @pl.when(t + 1 < nt)
    def _():
        runs(t + 1, 1 - slot, False)

    runs(t, slot, True)

    yb = _unpack_bf16_pairs(buf_ref[slot])
    col = lax.broadcasted_iota(I32, (tm, _sort_rows(tm)), 1)
    lp = lpos_ref[...]
    w = w_ref[...]
    wmat = jnp.zeros(col.shape, F32)
    for k in range(TOP_K):
        wmat = jnp.where(col == lp[:, k:k + 1], w[:, k:k + 1], wmat)
    w_hi = wmat.astype(BF16)
    w_lo = (wmat - w_hi.astype(F32)).astype(BF16)
    o_ref[...] = (h1_ref[...] + jnp.dot(w_hi, yb, preferred_element_type=F32)
                  + jnp.dot(w_lo, yb, preferred_element_type=F32))


def _combine(run_row, run_cnt, lpos_tok, w_tok, h1, ys):
    T = h1.shape[0]
    tm = min(ROUTE_TILE, T)
    return pl.pallas_call(
        _combine_body,
        grid_spec=pltpu.PrefetchScalarGridSpec(
            num_scalar_prefetch=2,
            grid=(T // tm,),
            in_specs=[
                pl.BlockSpec((tm, TOP_K), lambda i, *_: (i, 0)),
                pl.BlockSpec((tm, TOP_K), lambda i, *_: (i, 0)),
                pl.BlockSpec((tm, D_MODEL), lambda i, *_: (i, 0)),
                pl.BlockSpec(memory_space=pl.ANY),
            ],
            out_specs=pl.BlockSpec((tm, D_MODEL), lambda i, *_: (i, 0)),
            scratch_shapes=[
                pltpu.VMEM((2, _sort_rows(tm), PACKED_W), U32),
                pltpu.SemaphoreType.DMA((2,)),
            ],
        ),
        out_shape=jax.ShapeDtypeStruct((T, D_MODEL), F32),
        compiler_params=_params("arbitrary"),
        name="combine",
    )(run_row, run_cnt, lpos_tok, w_tok, h1, ys)


def _layer(h, l, p):
    B, S, _ = h.shape
    T = B * S
    lambda_init = 0.8 - 0.6 * math.exp(-0.3 * l)
    hf = h.reshape(T, D_MODEL)

    w_in = p["w_in"]
    sizes = [ATTN_WIDTH, ATTN_WIDTH, ATTN_WIDTH, D_INNER, CONV_DIM, SSM_HEADS, 2 * D_MODEL]
    offs = [0]
    for s in sizes:
        offs.append(offs[-1] + s)
    wq, wk, wv, wz, wxbc, wdt, wg = (w_in[:, offs[n]:offs[n + 1]] for n in range(7))
    w_main = jnp.concatenate([wq, wk, wv, wxbc, wz, wg], axis=1).astype(BF16)
    pad = jnp.zeros((D_MODEL, LANES - 3 * SSM_HEADS), F32)
    w_dt = jnp.concatenate([wdt, wdt, wdt, pad], axis=1).astype(BF16)

    proj, dt_raw = _in_proj(hf, p["norm1_w"].reshape(1, D_MODEL), w_main, w_dt)

    qk_scale = ATTN_HEAD_DIM ** -0.5 * LOG2E
    qg = (jnp.tile(p["q_norm_w"], 2) * qk_scale).reshape(1, LANES)
    kg = jnp.tile(p["k_norm_w"], 2).reshape(1, LANES)
    slopes = jnp.power(2.0, -8.0 * (jnp.arange(ATTN_HEADS, dtype=F32) + 1.0) / ATTN_HEADS) * LOG2E
    c_tab = jnp.broadcast_to(slopes[:, None, None], (ATTN_HEADS, 1, LANES))
    c_row_tab = jnp.broadcast_to(slopes[:, None, None], (ATTN_HEADS, 1, ATTN_TILE))
    lam = (jnp.exp(jnp.sum(p["lambda_q1"] * p["lambda_k1"]))
           - jnp.exp(jnp.sum(p["lambda_q2"] * p["lambda_k2"])) + lambda_init).reshape(1)
    qT, ka, vT = _attn_prep(proj, qg, kg, c_tab, B, S)
    attn = _attention(lam, qT, ka, vT, c_row_tab, p["subln_w"].reshape(LANES, 1), B, S, lambda_init)

    rep3 = lambda v: jnp.concatenate([v, v, v, jnp.zeros((LANES - 3 * SSM_HEADS,), F32)]).reshape(1, LANES)
    head_of_row = jnp.arange(LANES) % SSM_HEADS
    head_of_col = jnp.arange(D_INNER) // SSM_HEAD_DIM
    expand = ((head_of_row[:, None] == head_of_col[None, :])
              & (jnp.arange(LANES)[:, None] < 3 * SSM_HEADS)).astype(BF16)
    ssm = _ssd(proj, dt_raw, p["conv_w"], p["conv_b"].reshape(1, CONV_DIM), rep3(p["dt_bias"]),
               rep3(p["a_log"]), jnp.repeat(p["d_skip"], SSM_HEAD_DIM).reshape(1, D_INNER),
               p["ssm_norm_w"].reshape(1, D_INNER), expand, B, S)

    wr = jnp.pad(p["w_router"], ((0, 0), (0, LANES - N_EXPERTS))).astype(BF16)
    br = jnp.pad(p["b_router"], (0, LANES - N_EXPERTS)).reshape(1, LANES)
    h1, hn, logits = _merge(attn, ssm, proj, hf, p["w_attn_proj"].astype(BF16),
                            p["w_ssm_proj"].astype(BF16), p["w_out"].astype(BF16),
                            p["norm2_w"].reshape(1, D_MODEL), wr, br)

    idx, w_top, lpos = _route(logits)
    tr = min(ROUTE_TILE, T)
    experts = jnp.arange(N_EXPERTS, dtype=I32)
    chosen = idx.reshape(TOP_K, T // tr, tr)[..., None] == experts
    tile_cnt = jnp.sum(chosen, axis=(0, 2), dtype=I32)
    tile_cnt = (tile_cnt + ROW_ALIGN - 1) // ROW_ALIGN * ROW_ALIGN
    counts = jnp.sum(tile_cnt, axis=0)
    tiles_per = (counts + MOE_TILE - 1) // MOE_TILE
    tile_end = jnp.cumsum(tiles_per)
    group_start = (tile_end - tiles_per) * MOE_TILE
    n_runs = (T // tr) * N_EXPERTS
    n_tiles = -(-(T * TOP_K + n_runs * (ROW_ALIGN - 1)) // MOE_TILE) + N_EXPERTS
    M_pad = n_tiles * MOE_TILE
    tile_expert = jnp.minimum(jnp.sum(jnp.arange(n_tiles)[:, None] >= tile_end[None, :], axis=1),
                              N_EXPERTS - 1).astype(I32)
    n_used = tile_end[-1:].astype(I32)
    run_row = (group_start[None, :] + jnp.cumsum(tile_cnt, axis=0) - tile_cnt).reshape(-1)
    run_cnt = tile_cnt.reshape(-1)

    xs = _dispatch(run_row, run_cnt, group_start + counts, tiles_per * MOE_TILE - counts,
                   lpos, hn, M_pad)
    ys = _ffn(tile_expert, n_used, xs, p["w_gate_up"],
              p["b_gate_up"].reshape(N_EXPERTS, 1, 2 * D_FF), p["w_down"],
              p["b_down"].reshape(N_EXPERTS, 1, D_MODEL))
    out = _combine(run_row, run_cnt, lpos.T, w_top.T, h1, ys)
    return out.reshape(B, S, D_MODEL)


def kernel(x, norm1_w, w_in, q_norm_w, k_norm_w, lambda_q1, lambda_k1, lambda_q2, lambda_k2, subln_w, conv_w, conv_b, dt_bias, a_log, d_skip, ssm_norm_w, w_attn_proj, w_ssm_proj, w_out, norm2_w, w_router, b_router, w_gate_up, b_gate_up, w_down, b_down):
    params = dict(norm1_w=norm1_w, w_in=w_in, q_norm_w=q_norm_w, k_norm_w=k_norm_w,
                  lambda_q1=lambda_q1, lambda_k1=lambda_k1, lambda_q2=lambda_q2, lambda_k2=lambda_k2,
                  subln_w=subln_w, conv_w=conv_w, conv_b=conv_b, dt_bias=dt_bias, a_log=a_log,
                  d_skip=d_skip, ssm_norm_w=ssm_norm_w, w_attn_proj=w_attn_proj,
                  w_ssm_proj=w_ssm_proj, w_out=w_out, norm2_w=norm2_w, w_router=w_router,
                  b_router=b_router, w_gate_up=w_gate_up, b_gate_up=b_gate_up, w_down=w_down,
                  b_down=b_down)
    h = x
    for l in range(w_in.shape[0]):
        h = _layer(h, l, {k: v[l] for k, v in params.items()})
    return h
```

```python
import functools
import math

import jax
import jax.numpy as jnp
from jax import lax
from jax.experimental import pallas as pl
from jax.experimental.pallas import tpu as pltpu

F32, BF16, I32, U32 = jnp.float32, jnp.bfloat16, jnp.int32, jnp.uint32

D_MODEL = 1024
ATTN_HEADS = 8
ATTN_HEAD_DIM = 64
ATTN_WIDTH = ATTN_HEADS * 2 * ATTN_HEAD_DIM
SSM_EXPAND = 2
D_INNER = SSM_EXPAND * D_MODEL
SSM_HEAD_DIM = 64
SSM_HEADS = D_INNER // SSM_HEAD_DIM
SSM_GROUPS = 4
SSM_STATE = 128
CONV_WIDTH = 4
CONV_DIM = D_INNER + 2 * SSM_GROUPS * SSM_STATE
SSM_CHUNK = 128
N_EXPERTS = 32
TOP_K = 4
D_FF = D_MODEL
SWIGLU_LIMIT = 7.0
SWIGLU_ALPHA = 1.702
EPS = 1e-5
QK_EPS = 1e-6

LANES = 128
HEADS_PER_GROUP = SSM_HEADS // SSM_GROUPS
GROUP_WIDTH = D_INNER // SSM_GROUPS
LOG2E = math.log2(math.e)
NEG = -1e30
VMEM_LIMIT = 56 * 1024 * 1024

COL_Q, COL_K, COL_V = 0, ATTN_WIDTH, 2 * ATTN_WIDTH
COL_XBC = 3 * ATTN_WIDTH
COL_Z = COL_XBC + CONV_DIM
COL_GATE = COL_Z + D_INNER
PROJ_W = COL_GATE + 2 * D_MODEL

ATTN_TILE = 512
ATTN_HEADS_PER_STEP = 2
VT_ROWS = 2 * ATTN_HEAD_DIM + 16
MOE_TILE = 512
ROUTE_TILE = 256
CONV_HALO = 16
PACKED_W = D_MODEL // 2
ROW_ALIGN = 8
ROW_CHUNKS = (8, 16, 32, 64, 128, 256)


def _sort_rows(tm):
    return TOP_K * tm + N_EXPERTS * ROW_ALIGN


def _params(*sem):
    return pltpu.CompilerParams(dimension_semantics=sem, vmem_limit_bytes=VMEM_LIMIT)


def _sigmoid(x):
    return 0.5 * jnp.tanh(0.5 * x) + 0.5


def _silu(x):
    h = 0.5 * x
    return h + h * jnp.tanh(h)


def _pack_bf16_pairs(x):
    w = x.shape[1] // 2
    lo = lax.bitcast_convert_type(x[:, :w], U32) >> 16
    hi = lax.bitcast_convert_type(x[:, w:], U32) & jnp.uint32(0xFFFF0000)
    return hi | lo


def _unpack_bf16_pairs(p):
    lo = lax.bitcast_convert_type(p << 16, F32)
    hi = lax.bitcast_convert_type(p & jnp.uint32(0xFFFF0000), F32)
    return jnp.concatenate([lo, hi], axis=1).astype(BF16)


def _inproj_body(x_ref, g_ref, w_ref, wdt_ref, o_ref, dt_ref, u_scr):
    @pl.when(pl.program_id(1) == 0)
    def _():
        x = x_ref[...]
        u = x * lax.rsqrt(jnp.mean(x * x, axis=-1, keepdims=True) + EPS) * g_ref[...]
        ub = u.astype(BF16)
        u_scr[...] = ub
        dt_ref[...] = jnp.dot(ub, wdt_ref[...], preferred_element_type=F32)

    o_ref[...] = jnp.dot(u_scr[...], w_ref[...], preferred_element_type=F32).astype(o_ref.dtype)


def _in_proj(xf, gain, w_main, w_dt):
    T = xf.shape[0]
    tm = min(1024, T)
    tn = 2048
    return pl.pallas_call(
        _inproj_body,
        grid=(T // tm, PROJ_W // tn),
        in_specs=[
            pl.BlockSpec((tm, D_MODEL), lambda i, j: (i, 0)),
            pl.BlockSpec((1, D_MODEL), lambda i, j: (0, 0)),
            pl.BlockSpec((D_MODEL, tn), lambda i, j: (0, j)),
            pl.BlockSpec((D_MODEL, LANES), lambda i, j: (0, 0)),
        ],
        out_specs=[
            pl.BlockSpec((tm, tn), lambda i, j: (i, j)),
            pl.BlockSpec((tm, LANES), lambda i, j: (i, 0)),
        ],
        out_shape=[
            jax.ShapeDtypeStruct((T, PROJ_W), BF16),
            jax.ShapeDtypeStruct((T, LANES), F32),
        ],
        scratch_shapes=[pltpu.VMEM((tm, D_MODEL), BF16)],
        compiler_params=_params("parallel", "arbitrary"),
        name="in_proj",
    )(xf, gain, w_main, w_dt)


def _split3(x):
    hi = x.astype(BF16).astype(F32)
    r = x - hi
    mid = r.astype(BF16).astype(F32)
    return hi, mid, r - mid


def _attn_prep_body(q_ref, k_ref, v_ref, qg_ref, kg_ref, c_ref, qT_ref, ka_ref, vT_ref, *, tk):
    ts = q_ref.shape[0]
    lane = lax.broadcasted_iota(I32, (ts, LANES), 1)
    lo_half = lane < ATTN_HEAD_DIM

    def half_norm(x, g):
        x2 = x * x
        s_lo = jnp.sum(jnp.where(lo_half, x2, 0.0), axis=-1, keepdims=True)
        s_hi = jnp.sum(jnp.where(lo_half, 0.0, x2), axis=-1, keepdims=True)
        ms = jnp.where(lo_half, s_lo, s_hi) * (1.0 / ATTN_HEAD_DIM)
        return x * lax.rsqrt(ms + QK_EPS) * g

    q = half_norm(q_ref[...].astype(F32), qg_ref[...])
    k = half_norm(k_ref[...].astype(F32), kg_ref[...])

    row = lax.broadcasted_iota(I32, (ts, LANES), 0) + pl.program_id(2) * ts
    bias = c_ref[...] * (row % tk).astype(F32)
    b_hi, b_mid, b_lo = _split3(bias)
    k_tail = jnp.where(lane == ATTN_HEAD_DIM, b_hi,
                       jnp.where(lane == ATTN_HEAD_DIM + 1, b_mid,
                                 jnp.where(lane == ATTN_HEAD_DIM + 2, b_lo, 0.0)))
    qT = q.T
    tail_row = lax.broadcasted_iota(I32, (ATTN_HEAD_DIM, ts), 0)
    q_tail = jnp.where(tail_row < 3, 1.0, 0.0)

    for comp in range(2):
        kc = k if comp == 0 else pltpu.roll(k, ATTN_HEAD_DIM, 1)
        qc = qT[comp * ATTN_HEAD_DIM:(comp + 1) * ATTN_HEAD_DIM, :]
        qT_ref[comp] = jnp.concatenate([qc, q_tail], axis=0).astype(BF16)
        ka_ref[comp] = jnp.where(lo_half, kc, k_tail).astype(BF16)
    dv = 2 * ATTN_HEAD_DIM
    vT_ref[0:dv, :] = v_ref[...].astype(F32).T.astype(BF16)
    vT_ref[dv:VT_ROWS, :] = jnp.ones((VT_ROWS - dv, ts), BF16)


def _attn_prep(proj, qg, kg, c_tab, B, S):
    ts = min(1024, S)
    ns = S // ts
    H = ATTN_HEADS
    hb = ATTN_WIDTH // LANES
    return pl.pallas_call(
        functools.partial(_attn_prep_body, tk=ATTN_TILE),
        grid=(B, H, ns),
        in_specs=[
            pl.BlockSpec((ts, LANES), lambda b, h, s: (b * ns + s, h)),
            pl.BlockSpec((ts, LANES), lambda b, h, s: (b * ns + s, hb + h)),
            pl.BlockSpec((ts, LANES), lambda b, h, s: (b * ns + s, 2 * hb + h)),
            pl.BlockSpec((1, LANES), lambda b, h, s: (0, 0)),
            pl.BlockSpec((1, LANES), lambda b, h, s: (0, 0)),
            pl.BlockSpec((None, 1, LANES), lambda b, h, s: (h, 0, 0)),
        ],
        out_specs=[
            pl.BlockSpec((None, None, 2, LANES, ts), lambda b, h, s: (b, h, 0, 0, s)),
            pl.BlockSpec((None, None, 2, ts, LANES), lambda b, h, s: (b, h, 0, s, 0)),
            pl.BlockSpec((None, None, VT_ROWS, ts), lambda b, h, s: (b, h, 0, s)),
        ],
        out_shape=[
            jax.ShapeDtypeStruct((B, H, 2, LANES, S), BF16),
            jax.ShapeDtypeStruct((B, H, 2, S, LANES), BF16),
            jax.ShapeDtypeStruct((B, H, VT_ROWS, S), BF16),
        ],
        compiler_params=_params("parallel", "parallel", "parallel"),
        name="attn_prep",
    )(proj, proj, proj, qg, kg, c_tab)


def _attn_body(lam_ref, qT_ref, ka_ref, vT_ref, c_ref, w_ref, o_ref, *scratch, tile, out_scale):
    i = pl.program_id(2)
    streams = [(h, comp) for h in range(ATTN_HEADS_PER_STEP) for comp in range(2)]
    ns = len(streams)
    slot_a, slot_b, accs = scratch[:ns], scratch[ns:2 * ns], scratch[2 * ns:]
    for acc in accs:
        acc[...] = jnp.zeros_like(acc)
    c_rows = [c_ref[h] for h in range(ATTN_HEADS_PER_STEP)]

    def scores(j, slot):
        start = pl.multiple_of(j * tile, tile)
        for n, (h, comp) in enumerate(streams):
            slot[n][...] = jnp.dot(ka_ref[h, comp, pl.ds(start, tile), :], qT_ref[h, comp],
                                   preferred_element_type=F32)

    def softmax_pv(j, slot, ms, masked):
        start = pl.multiple_of(j * tile, tile)
        new_ms = []
        for n, (h, comp) in enumerate(streams):
            cb = c_rows[h] * ((j - i) * tile).astype(F32)
            sT = slot[n][...]
            if masked:
                key = lax.broadcasted_iota(I32, sT.shape, 0)
                qry = lax.broadcasted_iota(I32, sT.shape, 1)
                sT = jnp.where(key <= qry, sT, NEG)
            m_new = jnp.maximum(ms[n], jnp.max(sT, axis=0, keepdims=True) + cb)
            alpha = jnp.exp2(ms[n] - m_new)
            p = jnp.exp2(sT - (m_new - cb)).astype(BF16)
            pv = jnp.dot(vT_ref[h, :, pl.ds(start, tile)], p, preferred_element_type=F32)
            accs[n][...] = alpha * accs[n][...] + pv
            new_ms.append(m_new)
        return tuple(new_ms)

    def finish(slot, ms):
        softmax_pv(i, slot, ms, True)
        dv = 2 * ATTN_HEAD_DIM
        for h in range(ATTN_HEADS_PER_STEP):
            outs = [accs[2 * h + comp][0:dv, :] / accs[2 * h + comp][dv:dv + 1, :]
                    for comp in range(2)]
            o = outs[0] - lam_ref[0] * outs[1]
            msq = jnp.mean(o * o, axis=0, keepdims=True)
            y = o * lax.rsqrt(msq + EPS) * w_ref[...] * out_scale
            o_ref[:, h * dv:(h + 1) * dv] = y.T.astype(o_ref.dtype)

    def body(t, ms):
        j = 2 * t
        scores(j + 1, slot_b)
        ms = softmax_pv(j, slot_a, ms, False)
        scores(j + 2, slot_a)
        return softmax_pv(j + 1, slot_b, ms, False)

    scores(0, slot_a)
    m_init = jnp.full((1, tile), NEG, F32)
    ms = lax.fori_loop(0, i // 2, body, (m_init,) * ns)

    @pl.when(i % 2 == 0)
    def _():
        finish(slot_a, ms)

    @pl.when(i % 2 == 1)
    def _():
        scores(i, slot_b)
        finish(slot_b, softmax_pv(i - 1, slot_a, ms, False))


def _attention(lam, qT, ka, vT, c_row_tab, subln_col, B, S, lambda_init):
    tile = ATTN_TILE
    nh = ATTN_HEADS_PER_STEP
    nq = S // tile
    n_streams = 2 * nh
    return pl.pallas_call(
        functools.partial(_attn_body, tile=tile, out_scale=1.0 - lambda_init),
        grid=(B, ATTN_HEADS // nh, nq),
        in_specs=[
            pl.BlockSpec(memory_space=pltpu.SMEM),
            pl.BlockSpec((None, nh, 2, LANES, tile), lambda b, g, i: (b, g, 0, 0, i)),
            pl.BlockSpec((None, nh, 2, S, LANES), lambda b, g, i: (b, g, 0, 0, 0)),
            pl.BlockSpec((None, nh, VT_ROWS, S), lambda b, g, i: (b, g, 0, 0)),
            pl.BlockSpec((nh, 1, tile), lambda b, g, i: (g, 0, 0)),
            pl.BlockSpec((LANES, 1), lambda b, g, i: (0, 0)),
        ],
        out_specs=pl.BlockSpec((tile, nh * LANES), lambda b, g, i: (b * nq + i, g)),
        out_shape=jax.ShapeDtypeStruct((B * S, ATTN_WIDTH), BF16),
        scratch_shapes=([pltpu.VMEM((tile, tile), F32)] * (2 * n_streams)
                        + [pltpu.VMEM((VT_ROWS, tile), F32)] * n_streams),
        compiler_params=_params("parallel", "parallel", "arbitrary"),
        name="diff_attn",
    )(lam, qT, ka, vT, c_row_tab, subln_col)


def _expand_heads(x, e_ref):
    lane = lax.broadcasted_iota(I32, x.shape, 1)
    hi, mid, lo = _split3(x)
    parts = jnp.where(lane < SSM_HEADS, hi,
                      jnp.where(lane < 2 * SSM_HEADS, mid,
                                jnp.where(lane < 3 * SSM_HEADS, lo, 0.0)))
    return jnp.dot(parts.astype(BF16), e_ref[...], preferred_element_type=F32)


def _ssd_body(xbc_ref, z_ref, dt_ref, cw_ref, cb_ref, dtb_ref, alog_ref, dskip_ref, nw_ref, e_ref,
              o_ref, ext_ref, state_ref, y_ref):
    Q = SSM_CHUNK
    halo = CONV_HALO

    @pl.when(pl.program_id(1) == 0)
    def _():
        ext_ref[0:halo, :] = jnp.zeros((halo, CONV_DIM), BF16)
        state_ref[...] = jnp.zeros_like(state_ref)

    xb = xbc_ref[...]
    ext_ref[halo:halo + Q, :] = xb
    taps = CONV_WIDTH - 1
    out_row = lax.broadcasted_iota(I32, (taps * Q, halo + Q), 0)
    src_row = lax.broadcasted_iota(I32, (taps * Q, halo + Q), 1)
    shift = (src_row == out_row - (out_row // Q) * (Q - 1) + (halo - taps)).astype(BF16)
    shifted = jnp.dot(shift, ext_ref[...], preferred_element_type=F32)
    conv = cb_ref[...] + cw_ref[taps:taps + 1, :] * xb.astype(F32)
    for w in range(taps):
        conv = conv + cw_ref[w:w + 1, :] * shifted[w * Q:(w + 1) * Q, :]
    ext_ref[0:halo, :] = xb[Q - halo:Q, :]
    xc = _silu(conv)
    xs = xc[:, :D_INNER]

    raw = dt_ref[...] + dtb_ref[...]
    dt = jnp.maximum(raw, 0.0) + jnp.log2(1.0 + jnp.exp(-jnp.abs(raw))) * math.log(2.0)
    dA = dt * (-jnp.exp(alog_ref[...]))
    t_idx = lax.broadcasted_iota(I32, (Q, Q), 0)
    s_idx = lax.broadcasted_iota(I32, (Q, Q), 1)
    causal = s_idx <= t_idx
    cum = jnp.dot(causal.astype(F32), dA, preferred_element_type=F32,
                  precision=lax.Precision.HIGHEST)
    cumT = cum.T
    ecum = jnp.exp(cum)
    decay = jnp.exp(cum[Q - 1:Q, :] - cum)

    dt_e = _expand_heads(dt, e_ref)
    ecum_e = _expand_heads(ecum, e_ref)
    decay_e = _expand_heads(decay, e_ref)
    xdt = xs * dt_e
    xdt_b = xdt.astype(BF16)
    xdec_b = (xdt * decay_e).astype(BF16)

    lane = lax.broadcasted_iota(I32, (Q, LANES), 1)
    first_head = lane < SSM_HEAD_DIM
    for g in range(SSM_GROUPS):
        bcol = D_INNER + g * SSM_STATE
        ccol = D_INNER + SSM_GROUPS * SSM_STATE + g * SSM_STATE
        Bg = xc[:, bcol:bcol + SSM_STATE].astype(BF16)
        Cg = xc[:, ccol:ccol + SSM_STATE].astype(BF16)
        gs = slice(g * GROUP_WIDTH, (g + 1) * GROUP_WIDTH)
        CB = lax.dot_general(Cg, Bg, (((1,), (1,)), ((), ())), preferred_element_type=F32)
        state = state_ref[g]
        y_off = jnp.dot(Cg, state.astype(BF16), preferred_element_type=F32) * ecum_e[:, gs]
        for pair in range(HEADS_PER_GROUP // 2):
            h0 = g * HEADS_PER_GROUP + 2 * pair
            ms = []
            for h in (h0, h0 + 1):
                diff = cum[:, h:h + 1] - cumT[h:h + 1, :]
                L = jnp.exp(jnp.where(causal, diff, NEG))
                ms.append((CB * L).astype(BF16))
            lhs = jnp.concatenate(ms, axis=1)
            c0 = h0 * SSM_HEAD_DIM
            xp = xdt_b[:, c0:c0 + LANES]
            zero = jnp.zeros_like(xp)
            rhs = jnp.concatenate([jnp.where(first_head, xp, zero),
                                   jnp.where(first_head, zero, xp)], axis=0)
            y_ref[:, c0:c0 + LANES] = (jnp.dot(lhs, rhs, preferred_element_type=F32)
                                       + y_off[:, c0 - g * GROUP_WIDTH:c0 - g * GROUP_WIDTH + LANES])
        upd = lax.dot_general(Bg, xdec_b[:, gs], (((0,), (0,)), ((), ())),
                              preferred_element_type=F32)
        state_ref[g] = state * ecum_e[Q - 1:Q, gs] + upd

    y = y_ref[...] + xs * dskip_ref[...]
    z = z_ref[...].astype(F32)
    y = y * _silu(z)
    for g in range(SSM_GROUPS):
        gs = slice(g * GROUP_WIDTH, (g + 1) * GROUP_WIDTH)
        yg = y[:, gs]
        ms = jnp.mean(yg * yg, axis=-1, keepdims=True)
        o_ref[:, gs] = (yg * lax.rsqrt(ms + EPS) * nw_ref[:, gs]).astype(o_ref.dtype)


def _ssd(proj, dt_raw, conv_w, conv_b, dtb3, alog3, dskip_e, norm_w, expand, B, S):
    Q = SSM_CHUNK
    nc = S // Q
    row = lambda b, c: b * nc + c
    const = lambda b, c: (0, 0)
    return pl.pallas_call(
        _ssd_body,
        grid=(B, nc),
        in_specs=[
            pl.BlockSpec((Q, CONV_DIM), lambda b, c: (row(b, c), COL_XBC // CONV_DIM)),
            pl.BlockSpec((Q, D_INNER), lambda b, c: (row(b, c), COL_Z // D_INNER)),
            pl.BlockSpec((Q, LANES), lambda b, c: (row(b, c), 0)),
            pl.BlockSpec((CONV_WIDTH, CONV_DIM), const),
            pl.BlockSpec((1, CONV_DIM), const),
            pl.BlockSpec((1, LANES), const),
            pl.BlockSpec((1, LANES), const),
            pl.BlockSpec((1, D_INNER), const),
            pl.BlockSpec((1, D_INNER), const),
            pl.BlockSpec((LANES, D_INNER), const),
        ],
        out_specs=pl.BlockSpec((Q, D_INNER), lambda b, c: (row(b, c), 0)),
        out_shape=jax.ShapeDtypeStruct((B * S, D_INNER), BF16),
        scratch_shapes=[
            pltpu.VMEM((Q + CONV_HALO, CONV_DIM), BF16),
            pltpu.VMEM((SSM_GROUPS, SSM_STATE, GROUP_WIDTH), F32),
            pltpu.VMEM((Q, D_INNER), F32),
        ],
        compiler_params=_params("parallel", "arbitrary"),
        name="ssd",
    )(proj, proj, dt_raw, conv_w, conv_b, dtb3, alog3, dskip_e, norm_w, expand)


def _merge_body(attn_ref, ssm_ref, gate_ref, x_ref, wa_ref, ws_ref, wo_ref, n2_ref, wr_ref, br_ref,
                h1_ref, hn_ref, idx_ref, w_ref, lpos_ref):
    gate = gate_ref[...].astype(F32)
    ya = jnp.dot(attn_ref[...], wa_ref[...], preferred_element_type=F32)
    ys = jnp.dot(ssm_ref[...], ws_ref[...], preferred_element_type=F32)
    mixed = _sigmoid(gate[:, :D_MODEL]) * ya + _sigmoid(gate[:, D_MODEL:]) * ys
    h1 = x_ref[...] + jnp.dot(mixed.astype(BF16), wo_ref[...], preferred_element_type=F32)
    h1_ref[...] = h1
    hn = h1 * lax.rsqrt(jnp.mean(h1 * h1, axis=-1, keepdims=True) + EPS) * n2_ref[...]
    hn_b = hn.astype(BF16)
    hn_ref[...] = hn_b
    logits = jnp.dot(hn_b, wr_ref[...], preferred_element_type=F32) + br_ref[...]
    _route_tile(logits, idx_ref, w_ref, lpos_ref)


def _merge(attn, ssm, proj, xf, wa, ws, wo, n2, wr, br):
    T = xf.shape[0]
    tm = min(ROUTE_TILE, T)
    const = lambda i: (0, 0)
    route_spec = pl.BlockSpec((TOP_K, tm), lambda i: (0, i))
    return pl.pallas_call(
        _merge_body,
        grid=(T // tm,),
        in_specs=[
            pl.BlockSpec((tm, ATTN_WIDTH), lambda i: (i, 0)),
            pl.BlockSpec((tm, D_INNER), lambda i: (i, 0)),
            pl.BlockSpec((tm, 2 * D_MODEL), lambda i: (i, COL_GATE // (2 * D_MODEL))),
            pl.BlockSpec((tm, D_MODEL), lambda i: (i, 0)),
            pl.BlockSpec((ATTN_WIDTH, D_MODEL), const),
            pl.BlockSpec((D_INNER, D_MODEL), const),
            pl.BlockSpec((D_MODEL, D_MODEL), const),
            pl.BlockSpec((1, D_MODEL), const),
            pl.BlockSpec((D_MODEL, LANES), const),
            pl.BlockSpec((1, LANES), const),
        ],
        out_specs=[
            pl.BlockSpec((tm, D_MODEL), lambda i: (i, 0)),
            pl.BlockSpec((tm, D_MODEL), lambda i: (i, 0)),
            route_spec, route_spec, route_spec,
        ],
        out_shape=[
            jax.ShapeDtypeStruct((T, D_MODEL), F32),
            jax.ShapeDtypeStruct((T, D_MODEL), BF16),
            jax.ShapeDtypeStruct((TOP_K, T), I32),
            jax.ShapeDtypeStruct((TOP_K, T), F32),
            jax.ShapeDtypeStruct((TOP_K, T), I32),
        ],
        compiler_params=_params("parallel"),
        name="merge",
    )(attn, ssm, proj, xf, wa, ws, wo, n2, wr, br)


def _route_tile(logits, idx_ref, w_ref, lpos_ref):
    tm = logits.shape[0]
    logit = logits.T[:N_EXPERTS, :]
    eio = lax.broadcasted_iota(I32, logit.shape, 0)
    vals, hits = [], []
    for k in range(TOP_K):
        mx = jnp.max(logit, axis=0, keepdims=True)
        idx = jnp.min(jnp.where(logit == mx, eio, N_EXPERTS), axis=0, keepdims=True)
        hit = eio == idx
        logit = jnp.where(hit, -jnp.inf, logit)
        idx_ref[k:k + 1, :] = idx
        vals.append(mx)
        hits.append(hit)
    exps = [jnp.exp(v - vals[0]) for v in vals]
    denom = exps[0] + exps[1] + exps[2] + exps[3]
    for k in range(TOP_K):
        w_ref[k:k + 1, :] = exps[k] / denom

    sel = (hits[0] | hits[1] | hits[2] | hits[3]).astype(BF16)
    before = (lax.broadcasted_iota(I32, (tm, tm), 0) < lax.broadcasted_iota(I32, (tm, tm), 1))
    prefix = jnp.dot(sel, before.astype(BF16), preferred_element_type=F32)
    count = jnp.sum(sel.astype(F32), axis=1, keepdims=True)
    count = jnp.floor((count + (ROW_ALIGN - 1)) * (1.0 / ROW_ALIGN)) * ROW_ALIGN
    count = jnp.broadcast_to(count, (N_EXPERTS, LANES))
    lower = (lax.broadcasted_iota(I32, (N_EXPERTS, N_EXPERTS), 1)
             < lax.broadcasted_iota(I32, (N_EXPERTS, N_EXPERTS), 0))
    start = jnp.dot(lower.astype(BF16), count.astype(BF16), preferred_element_type=F32)[:, 0:1]
    offs = prefix + start
    for k in range(TOP_K):
        lpos_ref[k:k + 1, :] = jnp.sum(jnp.where(hits[k], offs, 0.0), axis=0,
                                       keepdims=True).astype(I32)


def _for_row_chunks(count, fn):
    for size in ROW_CHUNKS:
        @pl.when((count & size) != 0)
        def _():
            fn(count & (size - 1), size)


def _rows_copy(src_ref, src_row, dst_ref, dst_row, size, sem):
    return pltpu.make_async_copy(src_ref.at[pl.ds(pl.multiple_of(src_row, ROW_ALIGN), size), :],
                                 dst_ref.at[pl.ds(pl.multiple_of(dst_row, ROW_ALIGN), size), :], sem)


def _dispatch_body(row_ref, cnt_ref, pad_row_ref, pad_cnt_ref, nu_ref, lpos_ref, hn_ref, xs_ref,
                   sorted_ref, zero_ref, sems, pad_sem):
    t = pl.program_id(0)
    nt = pl.num_programs(0)
    tm = hn_ref.shape[0]
    slot = t % 2

    @pl.when(t == 0)
    def _():
        zero_ref[...] = jnp.zeros_like(zero_ref)
        big = ROW_CHUNKS[-1]
        for wait in (False, True):
            def pad(e, carry):
                def one(off, size):
                    cp = _rows_copy(zero_ref, 0, xs_ref, pad_row_ref[e] + off, size, pad_sem)
                    cp.wait() if wait else cp.start()
                _for_row_chunks(pad_cnt_ref[e], one)
                return carry
            lax.fori_loop(0, N_EXPERTS, pad, 0)

            def tail(k, carry):
                cp = _rows_copy(zero_ref, 0, xs_ref, k * big, big, pad_sem)
                cp.wait() if wait else cp.start()
                return carry
            lax.fori_loop(nu_ref[0] * (MOE_TILE // big), xs_ref.shape[0] // big, tail, 0)

    lp = lpos_ref[...]
    r = lax.broadcasted_iota(I32, (_sort_rows(tm), tm), 0)
    onehot = jnp.where(r == lp[0:1], 1.0, jnp.where(r == lp[1:2], 1.0,
             jnp.where(r == lp[2:3], 1.0, jnp.where(r == lp[3:4], 1.0, 0.0))))
    sorted_ref[slot] = _pack_bf16_pairs(jnp.dot(onehot.astype(BF16), hn_ref[...],
                                                preferred_element_type=F32))

    def runs(tile, buf_slot, wait):
        def one_expert(e, off):
            n = cnt_ref[tile * N_EXPERTS + e]
            dst = row_ref[tile * N_EXPERTS + e]
            def one(sub, size):
                cp = _rows_copy(sorted_ref.at[buf_slot], off + sub, xs_ref, dst + sub, size,
                                sems.at[buf_slot])
                cp.wait() if wait else cp.start()
            _for_row_chunks(n, one)
            return off + n
        lax.fori_loop(0, N_EXPERTS, one_expert, 0)

    runs(t, slot, False)

    @pl.when(t > 0)
    def _():
        runs(t - 1, 1 - slot, True)

    @pl.when(t == nt - 1)
    def _():
        runs(t, slot, True)


def _dispatch(run_row, run_cnt, pad_row, pad_cnt, n_used, lpos, hn, m_pad):
    T = hn.shape[0]
    tm = min(ROUTE_TILE, T)
    return pl.pallas_call(
        _dispatch_body,
        grid_spec=pltpu.PrefetchScalarGridSpec(
            num_scalar_prefetch=5,
            grid=(T // tm,),
            in_specs=[
                pl.BlockSpec((TOP_K, tm), lambda i, *_: (0, i)),
                pl.BlockSpec((tm, D_MODEL), lambda i, *_: (i, 0)),
            ],
            out_specs=pl.BlockSpec(memory_space=pl.ANY),
            scratch_shapes=[
                pltpu.VMEM((2, _sort_rows(tm), PACKED_W), U32),
                pltpu.VMEM((ROW_CHUNKS[-1], PACKED_W), U32),
                pltpu.SemaphoreType.DMA((2,)),
                pltpu.SemaphoreType.DMA(()),
            ],
        ),
        out_shape=jax.ShapeDtypeStruct((m_pad, PACKED_W), U32),
        compiler_params=_params("arbitrary"),
        name="dispatch",
    )(run_row, run_cnt, pad_row, pad_cnt, n_used, lpos, hn)


def _ffn_body(te_ref, nu_ref, xs_ref, wgu_ref, bgu_ref, wd_ref, bd_ref, ys_ref, wgu_bf, wd_bf):
    i = pl.program_id(0)

    @pl.when((i == 0) | (te_ref[i] != te_ref[jnp.maximum(i, 1) - 1]))
    def _():
        wgu_bf[...] = wgu_ref[...].astype(BF16)
        wd_bf[...] = wd_ref[...].astype(BF16)

    @pl.when(i >= nu_ref[0])
    def _():
        ys_ref[...] = jnp.zeros_like(ys_ref)

    @pl.when(i < nu_ref[0])
    def _():
        gu = jnp.dot(_unpack_bf16_pairs(xs_ref[...]), wgu_bf[...],
                     preferred_element_type=F32) + bgu_ref[...]
        glu = jnp.minimum(gu[:, :D_FF], SWIGLU_LIMIT)
        lin = jnp.clip(gu[:, D_FF:], -SWIGLU_LIMIT, SWIGLU_LIMIT)
        act = glu * _sigmoid(SWIGLU_ALPHA * glu) * (lin + 1.0)
        y = jnp.dot(act.astype(BF16), wd_bf[...], preferred_element_type=F32) + bd_ref[...]
        ys_ref[...] = _pack_bf16_pairs(y.astype(BF16).astype(F32))


def _ffn(tile_expert, n_used, xs, wgu, bgu, wd, bd):
    M_pad = xs.shape[0]
    tm = MOE_TILE
    n_tiles = M_pad // tm
    row = lambda i, te, nu: (jnp.minimum(i, nu[0] - 1), 0)
    return pl.pallas_call(
        _ffn_body,
        grid_spec=pltpu.PrefetchScalarGridSpec(
            num_scalar_prefetch=2,
            grid=(n_tiles,),
            in_specs=[
                pl.BlockSpec((tm, PACKED_W), row),
                pl.BlockSpec((None, D_MODEL, 2 * D_FF), lambda i, te, nu: (te[i], 0, 0)),
                pl.BlockSpec((None, 1, 2 * D_FF), lambda i, te, nu: (te[i], 0, 0)),
                pl.BlockSpec((None, D_FF, D_MODEL), lambda i, te, nu: (te[i], 0, 0)),
                pl.BlockSpec((None, 1, D_MODEL), lambda i, te, nu: (te[i], 0, 0)),
            ],
            out_specs=pl.BlockSpec((tm, PACKED_W), lambda i, te, nu: (i, 0)),
            scratch_shapes=[pltpu.VMEM((D_MODEL, 2 * D_FF), BF16), pltpu.VMEM((D_FF, D_MODEL), BF16)],
        ),
        out_shape=jax.ShapeDtypeStruct((M_pad, PACKED_W), U32),
        compiler_params=_params("arbitrary"),
        name="expert_ffn",
    )(tile_expert, n_used, xs, wgu, bgu, wd, bd)


def _combine_body(row_ref, cnt_ref, lpos_ref, w_ref, h1_ref, ys_ref, o_ref, buf_ref, sems):
    t = pl.program_id(0)
    nt = pl.num_programs(0)
    tm = h1_ref.shape[0]
    slot = t % 2

    def runs(tile, buf_slot, wait):
        def one_expert(e, off):
            n = cnt_ref[tile * N_EXPERTS + e]
            src = row_ref[tile * N_EXPERTS + e]
            def one(sub, size):
                cp = _rows_copy(ys_ref, src + sub, buf_ref.at[buf_slot], off + sub, size,
                                sems.at[buf_slot])
                cp.wait() if wait else cp.start()
            _for_row_chunks(n, one)
            return off + n
        lax.fori_loop(0, N_EXPERTS, one_expert, 0)

    @pl.when(t == 0)
    def _():
        buf_ref[...] = jnp.zeros_like(buf_ref)
        runs(t, slot, False)

    @pl.when(t + 1 < nt)
    def _():
        runs(t + 1, 1 - slot, False)

    runs(t, slot, True)

    yb = _unpack_bf16_pairs(buf_ref[slot])
    col = lax.broadcasted_iota(I32, (tm, _sort_rows(tm)), 1)
    lp = lpos_ref[...]
    w = w_ref[...]
    wmat = jnp.zeros(col.shape, F32)
    for k in range(TOP_K):
        wmat = jnp.where(col == lp[:, k:k + 1], w[:, k:k + 1], wmat)
    w_hi = wmat.astype(BF16)
    w_lo = (wmat - w_hi.astype(F32)).astype(BF16)
    o_ref[...] = (h1_ref[...] + jnp.dot(w_hi, yb, preferred_element_type=F32)
                  + jnp.dot(w_lo, yb, preferred_element_type=F32))


def _combine(run_row, run_cnt, lpos_tok, w_tok, h1, ys):
    T = h1.shape[0]
    tm = min(ROUTE_TILE, T)
    return pl.pallas_call(
        _combine_body,
        grid_spec=pltpu.PrefetchScalarGridSpec(
            num_scalar_prefetch=2,
            grid=(T // tm,),
            in_specs=[
                pl.BlockSpec((tm, TOP_K), lambda i, *_: (i, 0)),
                pl.BlockSpec((tm, TOP_K), lambda i, *_: (i, 0)),
                pl.BlockSpec((tm, D_MODEL), lambda i, *_: (i, 0)),
                pl.BlockSpec(memory_space=pl.ANY),
            ],
            out_specs=pl.BlockSpec((tm, D_MODEL), lambda i, *_: (i, 0)),
            scratch_shapes=[
                pltpu.VMEM((2, _sort_rows(tm), PACKED_W), U32),
                pltpu.SemaphoreType.DMA((2,)),
            ],
        ),
        out_shape=jax.ShapeDtypeStruct((T, D_MODEL), F32),
        compiler_params=_params("arbitrary"),
        name="combine",
    )(run_row, run_cnt, lpos_tok, w_tok, h1, ys)


def _layer(h, l, p):
    B, S, _ = h.shape
    T = B * S
    lambda_init = 0.8 - 0.6 * math.exp(-0.3 * l)
    hf = h.reshape(T, D_MODEL)

    w_in = p["w_in"]
    sizes = [ATTN_WIDTH, ATTN_WIDTH, ATTN_WIDTH, D_INNER, CONV_DIM, SSM_HEADS, 2 * D_MODEL]
    offs = [0]
    for s in sizes:
        offs.append(offs[-1] + s)
    wq, wk, wv, wz, wxbc, wdt, wg = (w_in[:, offs[n]:offs[n + 1]] for n in range(7))
    w_main = jnp.concatenate([wq, wk, wv, wxbc, wz, wg], axis=1).astype(BF16)
    pad = jnp.zeros((D_MODEL, LANES - 3 * SSM_HEADS), F32)
    w_dt = jnp.concatenate([wdt, wdt, wdt, pad], axis=1).astype(BF16)

    proj, dt_raw = _in_proj(hf, p["norm1_w"].reshape(1, D_MODEL), w_main, w_dt)

    qk_scale = ATTN_HEAD_DIM ** -0.5 * LOG2E
    qg = (jnp.tile(p["q_norm_w"], 2) * qk_scale).reshape(1, LANES)
    kg = jnp.tile(p["k_norm_w"], 2).reshape(1, LANES)
    slopes = jnp.power(2.0, -8.0 * (jnp.arange(ATTN_HEADS, dtype=F32) + 1.0) / ATTN_HEADS) * LOG2E
    c_tab = jnp.broadcast_to(slopes[:, None, None], (ATTN_HEADS, 1, LANES))
    c_row_tab = jnp.broadcast_to(slopes[:, None, None], (ATTN_HEADS, 1, ATTN_TILE))
    lam = (jnp.exp(jnp.sum(p["lambda_q1"] * p["lambda_k1"]))
           - jnp.exp(jnp.sum(p["lambda_q2"] * p["lambda_k2"])) + lambda_init).reshape(1)
    qT, ka, vT = _attn_prep(proj, qg, kg, c_tab, B, S)
    attn = _attention(lam, qT, ka, vT, c_row_tab, p["subln_w"].reshape(LANES, 1), B, S, lambda_init)

    rep3 = lambda v: jnp.concatenate([v, v, v, jnp.zeros((LANES - 3 * SSM_HEADS,), F32)]).reshape(1, LANES)
    head_of_row = jnp.arange(LANES) % SSM_HEADS
    head_of_col = jnp.arange(D_INNER) // SSM_HEAD_DIM
    expand = ((head_of_row[:, None] == head_of_col[None, :])
              & (jnp.arange(LANES)[:, None] < 3 * SSM_HEADS)).astype(BF16)
    ssm = _ssd(proj, dt_raw, p["conv_w"], p["conv_b"].reshape(1, CONV_DIM), rep3(p["dt_bias"]),
               rep3(p["a_log"]), jnp.repeat(p["d_skip"], SSM_HEAD_DIM).reshape(1, D_INNER),
               p["ssm_norm_w"].reshape(1, D_INNER), expand, B, S)

    wr = jnp.pad(p["w_router"], ((0, 0), (0, LANES - N_EXPERTS))).astype(BF16)
    br = jnp.pad(p["b_router"], (0, LANES - N_EXPERTS)).reshape(1, LANES)
    h1, hn, idx, w_top, lpos = _merge(attn, ssm, proj, hf, p["w_attn_proj"].astype(BF16),
                                      p["w_ssm_proj"].astype(BF16), p["w_out"].astype(BF16),
                                      p["norm2_w"].reshape(1, D_MODEL), wr, br)

    tr = min(ROUTE_TILE, T)
    experts = jnp.arange(N_EXPERTS, dtype=I32)
    chosen = idx.reshape(TOP_K, T // tr, tr)[..., None] == experts
    tile_cnt = jnp.sum(chosen, axis=(0, 2), dtype=I32)
    tile_cnt = (tile_cnt + ROW_ALIGN - 1) // ROW_ALIGN * ROW_ALIGN
    counts = jnp.sum(tile_cnt, axis=0)
    tiles_per = (counts + MOE_TILE - 1) // MOE_TILE
    tile_end = jnp.cumsum(tiles_per)
    group_start = (tile_end - tiles_per) * MOE_TILE
    n_runs = (T // tr) * N_EXPERTS
    n_tiles = -(-(T * TOP_K + n_runs * (ROW_ALIGN - 1)) // MOE_TILE) + N_EXPERTS
    M_pad = n_tiles * MOE_TILE
    tile_expert = jnp.minimum(jnp.sum(jnp.arange(n_tiles)[:, None] >= tile_end[None, :], axis=1),
                              N_EXPERTS - 1).astype(I32)
    n_used = tile_end[-1:].astype(I32)
    run_row = (group_start[None, :] + jnp.cumsum(tile_cnt, axis=0) - tile_cnt).reshape(-1)
    run_cnt = tile_cnt.reshape(-1)

    xs = _dispatch(run_row, run_cnt, group_start + counts, tiles_per * MOE_TILE - counts,
                   n_used, lpos, hn, M_pad)
    ys = _ffn(tile_expert, n_used, xs, p["w_gate_up"],
              p["b_gate_up"].reshape(N_EXPERTS, 1, 2 * D_FF), p["w_down"],
              p["b_down"].reshape(N_EXPERTS, 1, D_MODEL))
    out = _combine(run_row, run_cnt, lpos.T, w_top.T, h1, ys)
    return out.reshape(B, S, D_MODEL)


def kernel(x, norm1_w, w_in, q_norm_w, k_norm_w, lambda_q1, lambda_k1, lambda_q2, lambda_k2, subln_w, conv_w, conv_b, dt_bias, a_log, d_skip, ssm_norm_w, w_attn_proj, w_ssm_proj, w_out, norm2_w, w_router, b_router, w_gate_up, b_gate_up, w_down, b_down):
    params = dict(norm1_w=norm1_w, w_in=w_in, q_norm_w=q_norm_w, k_norm_w=k_norm_w,
                  lambda_q1=lambda_q1, lambda_k1=lambda_k1, lambda_q2=lambda_q2, lambda_k2=lambda_k2,
                  subln_w=subln_w, conv_w=conv_w, conv_b=conv_b, dt_bias=dt_bias, a_log=a_log,
                  d_skip=d_skip, ssm_norm_w=ssm_norm_w, w_attn_proj=w_attn_proj,
                  w_ssm_proj=w_ssm_proj, w_out=w_out, norm2_w=norm2_w, w_router=w_router,
                  b_router=b_router, w_gate_up=w_gate_up, b_gate_up=b_gate_up, w_down=w_down,
                  b_down=b_down)
    h = x
    for l in range(w_in.shape[0]):
        h = _layer(h, l, {k: v[l] for k, v in params.items()})
    return h
```

```python
import functools
import math

import jax
import jax.numpy as jnp
from jax import lax
from jax.experimental import pallas as pl
from jax.experimental.pallas import tpu as pltpu

F32, BF16, I32, U32 = jnp.float32, jnp.bfloat16, jnp.int32, jnp.uint32

D_MODEL = 1024
ATTN_HEADS = 8
ATTN_HEAD_DIM = 64
ATTN_WIDTH = ATTN_HEADS * 2 * ATTN_HEAD_DIM
SSM_EXPAND = 2
D_INNER = SSM_EXPAND * D_MODEL
SSM_HEAD_DIM = 64
SSM_HEADS = D_INNER // SSM_HEAD_DIM
SSM_GROUPS = 4
SSM_STATE = 128
CONV_WIDTH = 4
CONV_DIM = D_INNER + 2 * SSM_GROUPS * SSM_STATE
SSM_CHUNK = 128
N_EXPERTS = 32
TOP_K = 4
D_FF = D_MODEL
SWIGLU_LIMIT = 7.0
SWIGLU_ALPHA = 1.702
EPS = 1e-5
QK_EPS = 1e-6

LANES = 128
HEADS_PER_GROUP = SSM_HEADS // SSM_GROUPS
GROUP_WIDTH = D_INNER // SSM_GROUPS
LOG2E = math.log2(math.e)
NEG = -1e30
VMEM_LIMIT = 56 * 1024 * 1024

COL_Q, COL_K, COL_V = 0, ATTN_WIDTH, 2 * ATTN_WIDTH
COL_XBC = 3 * ATTN_WIDTH
COL_Z = COL_XBC + CONV_DIM
COL_GATE = COL_Z + D_INNER
PROJ_W = COL_GATE + 2 * D_MODEL

ATTN_TILE = 512
ATTN_HEADS_PER_STEP = 2
VT_ROWS = 2 * ATTN_HEAD_DIM + 16
MOE_TILE = 512
ROUTE_TILE = 256
CONV_HALO = 16
PACKED_W = D_MODEL // 2
ROW_ALIGN = 8
ROW_CHUNKS = (8, 16, 32, 64, 128, 256)


def _sort_rows(tm):
    return TOP_K * tm + N_EXPERTS * ROW_ALIGN


def _params(*sem):
    return pltpu.CompilerParams(dimension_semantics=sem, vmem_limit_bytes=VMEM_LIMIT)


def _sigmoid(x):
    return 0.5 * jnp.tanh(0.5 * x) + 0.5


def _silu(x):
    h = 0.5 * x
    return h + h * jnp.tanh(h)


def _pack_bf16_pairs(x):
    w = x.shape[1] // 2
    lo = lax.bitcast_convert_type(x[:, :w], U32) >> 16
    hi = lax.bitcast_convert_type(x[:, w:], U32) & jnp.uint32(0xFFFF0000)
    return hi | lo


def _unpack_bf16_pairs(p):
    lo = lax.bitcast_convert_type(p << 16, F32)
    hi = lax.bitcast_convert_type(p & jnp.uint32(0xFFFF0000), F32)
    return jnp.concatenate([lo, hi], axis=1).astype(BF16)


def _inproj_body(x_ref, g_ref, w_ref, wdt_ref, o_ref, dt_ref, u_scr):
    @pl.when(pl.program_id(1) == 0)
    def _():
        x = x_ref[...]
        u = x * lax.rsqrt(jnp.mean(x * x, axis=-1, keepdims=True) + EPS) * g_ref[...]
        ub = u.astype(BF16)
        u_scr[...] = ub
        dt_ref[...] = jnp.dot(ub, wdt_ref[...], preferred_element_type=F32)

    o_ref[...] = jnp.dot(u_scr[...], w_ref[...], preferred_element_type=F32).astype(o_ref.dtype)


def _in_proj(xf, gain, w_main, w_dt):
    T = xf.shape[0]
    tm = min(1024, T)
    tn = 2048
    return pl.pallas_call(
        _inproj_body,
        grid=(T // tm, PROJ_W // tn),
        in_specs=[
            pl.BlockSpec((tm, D_MODEL), lambda i, j: (i, 0)),
            pl.BlockSpec((1, D_MODEL), lambda i, j: (0, 0)),
            pl.BlockSpec((D_MODEL, tn), lambda i, j: (0, j)),
            pl.BlockSpec((D_MODEL, LANES), lambda i, j: (0, 0)),
        ],
        out_specs=[
            pl.BlockSpec((tm, tn), lambda i, j: (i, j)),
            pl.BlockSpec((tm, LANES), lambda i, j: (i, 0)),
        ],
        out_shape=[
            jax.ShapeDtypeStruct((T, PROJ_W), BF16),
            jax.ShapeDtypeStruct((T, LANES), F32),
        ],
        scratch_shapes=[pltpu.VMEM((tm, D_MODEL), BF16)],
        compiler_params=_params("parallel", "arbitrary"),
        name="in_proj",
    )(xf, gain, w_main, w_dt)


def _split3(x):
    hi = x.astype(BF16).astype(F32)
    r = x - hi
    mid = r.astype(BF16).astype(F32)
    return hi, mid, r - mid


def _attn_prep_body(q_ref, k_ref, v_ref, qg_ref, kg_ref, c_ref, qT_ref, ka_ref, vT_ref, *, tk):
    ts = q_ref.shape[0]
    lane = lax.broadcasted_iota(I32, (ts, LANES), 1)
    lo_half = lane < ATTN_HEAD_DIM

    def half_norm(x, g):
        x2 = x * x
        s_lo = jnp.sum(jnp.where(lo_half, x2, 0.0), axis=-1, keepdims=True)
        s_hi = jnp.sum(jnp.where(lo_half, 0.0, x2), axis=-1, keepdims=True)
        ms = jnp.where(lo_half, s_lo, s_hi) * (1.0 / ATTN_HEAD_DIM)
        return x * lax.rsqrt(ms + QK_EPS) * g

    q = half_norm(q_ref[...].astype(F32), qg_ref[...])
    k = half_norm(k_ref[...].astype(F32), kg_ref[...])

    row = lax.broadcasted_iota(I32, (ts, LANES), 0) + pl.program_id(2) * ts
    bias = c_ref[...] * (row % tk).astype(F32)
    b_hi, b_mid, b_lo = _split3(bias)
    k_tail = jnp.where(lane == ATTN_HEAD_DIM, b_hi,
                       jnp.where(lane == ATTN_HEAD_DIM + 1, b_mid,
                                 jnp.where(lane == ATTN_HEAD_DIM + 2, b_lo, 0.0)))
    qT = q.T
    tail_row = lax.broadcasted_iota(I32, (ATTN_HEAD_DIM, ts), 0)
    q_tail = jnp.where(tail_row < 3, 1.0, 0.0)

    for comp in range(2):
        kc = k if comp == 0 else pltpu.roll(k, ATTN_HEAD_DIM, 1)
        qc = qT[comp * ATTN_HEAD_DIM:(comp + 1) * ATTN_HEAD_DIM, :]
        qT_ref[comp] = jnp.concatenate([qc, q_tail], axis=0).astype(BF16)
        ka_ref[comp] = jnp.where(lo_half, kc, k_tail).astype(BF16)
    dv = 2 * ATTN_HEAD_DIM
    vT_ref[0:dv, :] = v_ref[...].astype(F32).T.astype(BF16)
    vT_ref[dv:VT_ROWS, :] = jnp.ones((VT_ROWS - dv, ts), BF16)


def _attn_prep(proj, qg, kg, c_tab, B, S):
    ts = min(1024, S)
    ns = S // ts
    H = ATTN_HEADS
    hb = ATTN_WIDTH // LANES
    return pl.pallas_call(
        functools.partial(_attn_prep_body, tk=ATTN_TILE),
        grid=(B, H, ns),
        in_specs=[
            pl.BlockSpec((ts, LANES), lambda b, h, s: (b * ns + s, h)),
            pl.BlockSpec((ts, LANES), lambda b, h, s: (b * ns + s, hb + h)),
            pl.BlockSpec((ts, LANES), lambda b, h, s: (b * ns + s, 2 * hb + h)),
            pl.BlockSpec((1, LANES), lambda b, h, s: (0, 0)),
            pl.BlockSpec((1, LANES), lambda b, h, s: (0, 0)),
            pl.BlockSpec((None, 1, LANES), lambda b, h, s: (h, 0, 0)),
        ],
        out_specs=[
            pl.BlockSpec((None, None, 2, LANES, ts), lambda b, h, s: (b, h, 0, 0, s)),
            pl.BlockSpec((None, None, 2, ts, LANES), lambda b, h, s: (b, h, 0, s, 0)),
            pl.BlockSpec((None, None, VT_ROWS, ts), lambda b, h, s: (b, h, 0, s)),
        ],
        out_shape=[
            jax.ShapeDtypeStruct((B, H, 2, LANES, S), BF16),
            jax.ShapeDtypeStruct((B, H, 2, S, LANES), BF16),
            jax.ShapeDtypeStruct((B, H, VT_ROWS, S), BF16),
        ],
        compiler_params=_params("parallel", "parallel", "parallel"),
        name="attn_prep",
    )(proj, proj, proj, qg, kg, c_tab)


def _attn_body(lam_ref, qT_ref, ka_ref, vT_ref, c_ref, w_ref, o_ref, *scratch, tile, out_scale):
    i = pl.program_id(2)
    streams = [(h, comp) for h in range(ATTN_HEADS_PER_STEP) for comp in range(2)]
    ns = len(streams)
    slot_a, slot_b, accs = scratch[:ns], scratch[ns:2 * ns], scratch[2 * ns:]
    for acc in accs:
        acc[...] = jnp.zeros_like(acc)
    c_rows = [c_ref[h] for h in range(ATTN_HEADS_PER_STEP)]

    def scores(j, slot):
        start = pl.multiple_of(j * tile, tile)
        for n, (h, comp) in enumerate(streams):
            slot[n][...] = jnp.dot(ka_ref[h, comp, pl.ds(start, tile), :], qT_ref[h, comp],
                                   preferred_element_type=F32)

    def softmax_pv(j, slot, ms, masked):
        start = pl.multiple_of(j * tile, tile)
        new_ms = []
        for n, (h, comp) in enumerate(streams):
            cb = c_rows[h] * ((j - i) * tile).astype(F32)
            sT = slot[n][...]
            if masked:
                key = lax.broadcasted_iota(I32, sT.shape, 0)
                qry = lax.broadcasted_iota(I32, sT.shape, 1)
                sT = jnp.where(key <= qry, sT, NEG)
            m_new = jnp.maximum(ms[n], jnp.max(sT, axis=0, keepdims=True) + cb)
            alpha = jnp.exp2(ms[n] - m_new)
            p = jnp.exp2(sT - (m_new - cb)).astype(BF16)
            pv = jnp.dot(vT_ref[h, :, pl.ds(start, tile)], p, preferred_element_type=F32)
            accs[n][...] = alpha * accs[n][...] + pv
            new_ms.append(m_new)
        return tuple(new_ms)

    def finish(slot, ms):
        softmax_pv(i, slot, ms, True)
        dv = 2 * ATTN_HEAD_DIM
        for h in range(ATTN_HEADS_PER_STEP):
            outs = [accs[2 * h + comp][0:dv, :] / accs[2 * h + comp][dv:dv + 1, :]
                    for comp in range(2)]
            o = outs[0] - lam_ref[0] * outs[1]
            msq = jnp.mean(o * o, axis=0, keepdims=True)
            y = o * lax.rsqrt(msq + EPS) * w_ref[...] * out_scale
            o_ref[:, h * dv:(h + 1) * dv] = y.T.astype(o_ref.dtype)

    def body(t, ms):
        j = 2 * t
        scores(j + 1, slot_b)
        ms = softmax_pv(j, slot_a, ms, False)
        scores(j + 2, slot_a)
        return softmax_pv(j + 1, slot_b, ms, False)

    scores(0, slot_a)
    m_init = jnp.full((1, tile), NEG, F32)
    ms = lax.fori_loop(0, i // 2, body, (m_init,) * ns)

    @pl.when(i % 2 == 0)
    def _():
        finish(slot_a, ms)

    @pl.when(i % 2 == 1)
    def _():
        scores(i, slot_b)
        finish(slot_b, softmax_pv(i - 1, slot_a, ms, False))


def _attention(lam, qT, ka, vT, c_row_tab, subln_col, B, S, lambda_init):
    tile = ATTN_TILE
    nh = ATTN_HEADS_PER_STEP
    nq = S // tile
    n_streams = 2 * nh
    return pl.pallas_call(
        functools.partial(_attn_body, tile=tile, out_scale=1.0 - lambda_init),
        grid=(B, ATTN_HEADS // nh, nq),
        in_specs=[
            pl.BlockSpec(memory_space=pltpu.SMEM),
            pl.BlockSpec((None, nh, 2, LANES, tile), lambda b, g, i: (b, g, 0, 0, i)),
            pl.BlockSpec((None, nh, 2, S, LANES), lambda b, g, i: (b, g, 0, 0, 0)),
            pl.BlockSpec((None, nh, VT_ROWS, S), lambda b, g, i: (b, g, 0, 0)),
            pl.BlockSpec((nh, 1, tile), lambda b, g, i: (g, 0, 0)),
            pl.BlockSpec((LANES, 1), lambda b, g, i: (0, 0)),
        ],
        out_specs=pl.BlockSpec((tile, nh * LANES), lambda b, g, i: (b * nq + i, g)),
        out_shape=jax.ShapeDtypeStruct((B * S, ATTN_WIDTH), BF16),
        scratch_shapes=([pltpu.VMEM((tile, tile), F32)] * (2 * n_streams)
                        + [pltpu.VMEM((VT_ROWS, tile), F32)] * n_streams),
        compiler_params=_params("parallel", "parallel", "arbitrary"),
        name="diff_attn",
    )(lam, qT, ka, vT, c_row_tab, subln_col)


def _expand_heads(x, e_ref):
    lane = lax.broadcasted_iota(I32, x.shape, 1)
    hi, mid, lo = _split3(x)
    parts = jnp.where(lane < SSM_HEADS, hi,
                      jnp.where(lane < 2 * SSM_HEADS, mid,
                                jnp.where(lane < 3 * SSM_HEADS, lo, 0.0)))
    return jnp.dot(parts.astype(BF16), e_ref[...], preferred_element_type=F32)


def _ssd_body(xbc_ref, z_ref, dt_ref, cw_ref, cb_ref, dtb_ref, alog_ref, dskip_ref, nw_ref, e_ref,
              o_ref, ext_ref, state_ref, y_ref):
    Q = SSM_CHUNK
    halo = CONV_HALO

    @pl.when(pl.program_id(1) == 0)
    def _():
        ext_ref[0:halo, :] = jnp.zeros((halo, CONV_DIM), BF16)
        state_ref[...] = jnp.zeros_like(state_ref)

    xb = xbc_ref[...]
    ext_ref[halo:halo + Q, :] = xb
    taps = CONV_WIDTH - 1
    out_row = lax.broadcasted_iota(I32, (taps * Q, halo + Q), 0)
    src_row = lax.broadcasted_iota(I32, (taps * Q, halo + Q), 1)
    shift = (src_row == out_row - (out_row // Q) * (Q - 1) + (halo - taps)).astype(BF16)
    shifted = jnp.dot(shift, ext_ref[...], preferred_element_type=F32)
    conv = cb_ref[...] + cw_ref[taps:taps + 1, :] * xb.astype(F32)
    for w in range(taps):
        conv = conv + cw_ref[w:w + 1, :] * shifted[w * Q:(w + 1) * Q, :]
    ext_ref[0:halo, :] = xb[Q - halo:Q, :]
    xc = _silu(conv)
    xs = xc[:, :D_INNER]

    raw = dt_ref[...] + dtb_ref[...]
    dt = jnp.maximum(raw, 0.0) + jnp.log2(1.0 + jnp.exp(-jnp.abs(raw))) * math.log(2.0)
    dA = dt * (-jnp.exp(alog_ref[...]))
    t_idx = lax.broadcasted_iota(I32, (Q, Q), 0)
    s_idx = lax.broadcasted_iota(I32, (Q, Q), 1)
    causal = s_idx <= t_idx
    cum = jnp.dot(causal.astype(F32), dA, preferred_element_type=F32,
                  precision=lax.Precision.HIGHEST)
    cumT = cum.T
    ecum = jnp.exp(cum)
    decay = jnp.exp(cum[Q - 1:Q, :] - cum)

    dt_e = _expand_heads(dt, e_ref)
    ecum_e = _expand_heads(ecum, e_ref)
    decay_e = _expand_heads(decay, e_ref)
    xdt = xs * dt_e
    xdt_b = xdt.astype(BF16)
    xdec_b = (xdt * decay_e).astype(BF16)

    lane = lax.broadcasted_iota(I32, (Q, LANES), 1)
    first_head = lane < SSM_HEAD_DIM
    for g in range(SSM_GROUPS):
        bcol = D_INNER + g * SSM_STATE
        ccol = D_INNER + SSM_GROUPS * SSM_STATE + g * SSM_STATE
        Bg = xc[:, bcol:bcol + SSM_STATE].astype(BF16)
        Cg = xc[:, ccol:ccol + SSM_STATE].astype(BF16)
        gs = slice(g * GROUP_WIDTH, (g + 1) * GROUP_WIDTH)
        CB = lax.dot_general(Cg, Bg, (((1,), (1,)), ((), ())), preferred_element_type=F32)
        state = state_ref[g]
        y_off = jnp.dot(Cg, state.astype(BF16), preferred_element_type=F32) * ecum_e[:, gs]
        for pair in range(HEADS_PER_GROUP // 2):
            h0 = g * HEADS_PER_GROUP + 2 * pair
            ms = []
            for h in (h0, h0 + 1):
                diff = cum[:, h:h + 1] - cumT[h:h + 1, :]
                L = jnp.exp(jnp.where(causal, diff, NEG))
                ms.append((CB * L).astype(BF16))
            lhs = jnp.concatenate(ms, axis=1)
            c0 = h0 * SSM_HEAD_DIM
            xp = xdt_b[:, c0:c0 + LANES]
            zero = jnp.zeros_like(xp)
            rhs = jnp.concatenate([jnp.where(first_head, xp, zero),
                                   jnp.where(first_head, zero, xp)], axis=0)
            y_ref[:, c0:c0 + LANES] = (jnp.dot(lhs, rhs, preferred_element_type=F32)
                                       + y_off[:, c0 - g * GROUP_WIDTH:c0 - g * GROUP_WIDTH + LANES])
        upd = lax.dot_general(Bg, xdec_b[:, gs], (((0,), (0,)), ((), ())),
                              preferred_element_type=F32)
        state_ref[g] = state * ecum_e[Q - 1:Q, gs] + upd

    y = y_ref[...] + xs * dskip_ref[...]
    z = z_ref[...].astype(F32)
    y = y * _silu(z)
    for g in range(SSM_GROUPS):
        gs = slice(g * GROUP_WIDTH, (g + 1) * GROUP_WIDTH)
        yg = y[:, gs]
        ms = jnp.mean(yg * yg, axis=-1, keepdims=True)
        o_ref[:, gs] = (yg * lax.rsqrt(ms + EPS) * nw_ref[:, gs]).astype(o_ref.dtype)


def _ssd(proj, dt_raw, conv_w, conv_b, dtb3, alog3, dskip_e, norm_w, expand, B, S):
    Q = SSM_CHUNK
    nc = S // Q
    row = lambda b, c: b * nc + c
    const = lambda b, c: (0, 0)
    return pl.pallas_call(
        _ssd_body,
        grid=(B, nc),
        in_specs=[
            pl.BlockSpec((Q, CONV_DIM), lambda b, c: (row(b, c), COL_XBC // CONV_DIM)),
            pl.BlockSpec((Q, D_INNER), lambda b, c: (row(b, c), COL_Z // D_INNER)),
            pl.BlockSpec((Q, LANES), lambda b, c: (row(b, c), 0)),
            pl.BlockSpec((CONV_WIDTH, CONV_DIM), const),
            pl.BlockSpec((1, CONV_DIM), const),
            pl.BlockSpec((1, LANES), const),
            pl.BlockSpec((1, LANES), const),
            pl.BlockSpec((1, D_INNER), const),
            pl.BlockSpec((1, D_INNER), const),
            pl.BlockSpec((LANES, D_INNER), const),
        ],
        out_specs=pl.BlockSpec((Q, D_INNER), lambda b, c: (row(b, c), 0)),
        out_shape=jax.ShapeDtypeStruct((B * S, D_INNER), BF16),
        scratch_shapes=[
            pltpu.VMEM((Q + CONV_HALO, CONV_DIM), BF16),
            pltpu.VMEM((SSM_GROUPS, SSM_STATE, GROUP_WIDTH), F32),
            pltpu.VMEM((Q, D_INNER), F32),
        ],
        compiler_params=_params("parallel", "arbitrary"),
        name="ssd",
    )(proj, proj, dt_raw, conv_w, conv_b, dtb3, alog3, dskip_e, norm_w, expand)


def _merge_body(attn_ref, ssm_ref, gate_ref, x_ref, wa_ref, ws_ref, wo_ref, n2_ref, wr_ref, br_ref,
                h1_ref, hn_ref, idx_ref, w_ref, lpos_ref):
    gate = gate_ref[...].astype(F32)
    ya = jnp.dot(attn_ref[...], wa_ref[...], preferred_element_type=F32)
    ys = jnp.dot(ssm_ref[...], ws_ref[...], preferred_element_type=F32)
    mixed = _sigmoid(gate[:, :D_MODEL]) * ya + _sigmoid(gate[:, D_MODEL:]) * ys
    h1 = x_ref[...] + jnp.dot(mixed.astype(BF16), wo_ref[...], preferred_element_type=F32)
    h1_ref[...] = h1
    hn = h1 * lax.rsqrt(jnp.mean(h1 * h1, axis=-1, keepdims=True) + EPS) * n2_ref[...]
    hn_b = hn.astype(BF16)
    hn_ref[...] = hn_b
    logits = jnp.dot(hn_b, wr_ref[...], preferred_element_type=F32) + br_ref[...]
    _route_tile(logits, idx_ref, w_ref, lpos_ref)


def _merge(attn, ssm, proj, xf, wa, ws, wo, n2, wr, br):
    T = xf.shape[0]
    tm = min(ROUTE_TILE, T)
    const = lambda i: (0, 0)
    route_spec = pl.BlockSpec((TOP_K, tm), lambda i: (0, i))
    return pl.pallas_call(
        _merge_body,
        grid=(T // tm,),
        in_specs=[
            pl.BlockSpec((tm, ATTN_WIDTH), lambda i: (i, 0)),
            pl.BlockSpec((tm, D_INNER), lambda i: (i, 0)),
            pl.BlockSpec((tm, 2 * D_MODEL), lambda i: (i, COL_GATE // (2 * D_MODEL))),
            pl.BlockSpec((tm, D_MODEL), lambda i: (i, 0)),
            pl.BlockSpec((ATTN_WIDTH, D_MODEL), const),
            pl.BlockSpec((D_INNER, D_MODEL), const),
            pl.BlockSpec((D_MODEL, D_MODEL), const),
            pl.BlockSpec((1, D_MODEL), const),
            pl.BlockSpec((D_MODEL, LANES), const),
            pl.BlockSpec((1, LANES), const),
        ],
        out_specs=[
            pl.BlockSpec((tm, D_MODEL), lambda i: (i, 0)),
            pl.BlockSpec((tm, D_MODEL), lambda i: (i, 0)),
            route_spec, route_spec, route_spec,
        ],
        out_shape=[
            jax.ShapeDtypeStruct((T, D_MODEL), F32),
            jax.ShapeDtypeStruct((T, D_MODEL), BF16),
            jax.ShapeDtypeStruct((TOP_K, T), I32),
            jax.ShapeDtypeStruct((TOP_K, T), F32),
            jax.ShapeDtypeStruct((TOP_K, T), I32),
        ],
        compiler_params=_params("parallel"),
        name="merge",
    )(attn, ssm, proj, xf, wa, ws, wo, n2, wr, br)


def _route_tile(logits, idx_ref, w_ref, lpos_ref):
    tm = logits.shape[0]
    logit = logits.T[:N_EXPERTS, :]
    eio = lax.broadcasted_iota(I32, logit.shape, 0)
    vals, hits = [], []
    for k in range(TOP_K):
        mx = jnp.max(logit, axis=0, keepdims=True)
        idx = jnp.min(jnp.where(logit == mx, eio, N_EXPERTS), axis=0, keepdims=True)
        hit = eio == idx
        logit = jnp.where(hit, -jnp.inf, logit)
        idx_ref[k:k + 1, :] = idx
        vals.append(mx)
        hits.append(hit)
    exps = [jnp.exp(v - vals[0]) for v in vals]
    denom = exps[0] + exps[1] + exps[2] + exps[3]
    for k in range(TOP_K):
        w_ref[k:k + 1, :] = exps[k] / denom

    sel = (hits[0] | hits[1] | hits[2] | hits[3]).astype(BF16)
    before = (lax.broadcasted_iota(I32, (tm, tm), 0) < lax.broadcasted_iota(I32, (tm, tm), 1))
    prefix = jnp.dot(sel, before.astype(BF16), preferred_element_type=F32)
    count = jnp.sum(sel.astype(F32), axis=1, keepdims=True)
    count = jnp.floor((count + (ROW_ALIGN - 1)) * (1.0 / ROW_ALIGN)) * ROW_ALIGN
    count = jnp.broadcast_to(count, (N_EXPERTS, LANES))
    lower = (lax.broadcasted_iota(I32, (N_EXPERTS, N_EXPERTS), 1)
             < lax.broadcasted_iota(I32, (N_EXPERTS, N_EXPERTS), 0))
    start = jnp.dot(lower.astype(BF16), count.astype(BF16), preferred_element_type=F32)[:, 0:1]
    offs = prefix + start
    for k in range(TOP_K):
        lpos_ref[k:k + 1, :] = jnp.sum(jnp.where(hits[k], offs, 0.0), axis=0,
                                       keepdims=True).astype(I32)


def _for_row_chunks(count, fn):
    for size in ROW_CHUNKS:
        @pl.when((count & size) != 0)
        def _():
            fn(count & (size - 1), size)


def _rows_copy(src_ref, src_row, dst_ref, dst_row, size, sem):
    return pltpu.make_async_copy(src_ref.at[pl.ds(pl.multiple_of(src_row, ROW_ALIGN), size), :],
                                 dst_ref.at[pl.ds(pl.multiple_of(dst_row, ROW_ALIGN), size), :], sem)


def _wait_rows(total, base_rows, src_ref, dst_ref, sem):
    _rows_copy(src_ref, 0, dst_ref, 0, base_rows, sem).wait()
    _for_row_chunks(total - base_rows,
                    lambda off, size: _rows_copy(src_ref, 0, dst_ref, 0, size, sem).wait())


def _dispatch_body(row_ref, cnt_ref, tot_ref, pad_row_ref, pad_cnt_ref, nu_ref, lpos_ref, hn_ref, xs_ref,
                   sorted_ref, zero_ref, sems, pad_sem):
    t = pl.program_id(0)
    nt = pl.num_programs(0)
    tm = hn_ref.shape[0]
    slot = t % 2

    @pl.when(t == 0)
    def _():
        zero_ref[...] = jnp.zeros_like(zero_ref)
        big = ROW_CHUNKS[-1]
        for wait in (False, True):
            def pad(e, carry):
                def one(off, size):
                    cp = _rows_copy(zero_ref, 0, xs_ref, pad_row_ref[e] + off, size, pad_sem)
                    cp.wait() if wait else cp.start()
                _for_row_chunks(pad_cnt_ref[e], one)
                return carry
            lax.fori_loop(0, N_EXPERTS, pad, 0)

            def tail(k, carry):
                cp = _rows_copy(zero_ref, 0, xs_ref, k * big, big, pad_sem)
                cp.wait() if wait else cp.start()
                return carry
            lax.fori_loop(nu_ref[0] * (MOE_TILE // big), xs_ref.shape[0] // big, tail, 0)

    lp = lpos_ref[...]
    r = lax.broadcasted_iota(I32, (_sort_rows(tm), tm), 0)
    onehot = jnp.where(r == lp[0:1], 1.0, jnp.where(r == lp[1:2], 1.0,
             jnp.where(r == lp[2:3], 1.0, jnp.where(r == lp[3:4], 1.0, 0.0))))
    sorted_ref[slot] = _pack_bf16_pairs(jnp.dot(onehot.astype(BF16), hn_ref[...],
                                                preferred_element_type=F32))

    def runs(tile, buf_slot, wait):
        def one_expert(e, off):
            n = cnt_ref[tile * N_EXPERTS + e]
            dst = row_ref[tile * N_EXPERTS + e]
            def one(sub, size):
                cp = _rows_copy(sorted_ref.at[buf_slot], off + sub, xs_ref, dst + sub, size,
                                sems.at[buf_slot])
                cp.wait() if wait else cp.start()
            _for_row_chunks(n, one)
            return off + n
        lax.fori_loop(0, N_EXPERTS, one_expert, 0)

    runs(t, slot, False)

    @pl.when(t > 0)
    def _():
        _wait_rows(tot_ref[t - 1], TOP_K * tm, sorted_ref.at[1 - slot], xs_ref, sems.at[1 - slot])

    @pl.when(t == nt - 1)
    def _():
        _wait_rows(tot_ref[t], TOP_K * tm, sorted_ref.at[slot], xs_ref, sems.at[slot])


def _dispatch(run_row, run_cnt, run_tot, pad_row, pad_cnt, n_used, lpos, hn, m_pad):
    T = hn.shape[0]
    tm = min(ROUTE_TILE, T)
    return pl.pallas_call(
        _dispatch_body,
        grid_spec=pltpu.PrefetchScalarGridSpec(
            num_scalar_prefetch=6,
            grid=(T // tm,),
            in_specs=[
                pl.BlockSpec((TOP_K, tm), lambda i, *_: (0, i)),
                pl.BlockSpec((tm, D_MODEL), lambda i, *_: (i, 0)),
            ],
            out_specs=pl.BlockSpec(memory_space=pl.ANY),
            scratch_shapes=[
                pltpu.VMEM((2, _sort_rows(tm), PACKED_W), U32),
                pltpu.VMEM((ROW_CHUNKS[-1], PACKED_W), U32),
                pltpu.SemaphoreType.DMA((2,)),
                pltpu.SemaphoreType.DMA(()),
            ],
        ),
        out_shape=jax.ShapeDtypeStruct((m_pad, PACKED_W), U32),
        compiler_params=_params("arbitrary"),
        name="dispatch",
    )(run_row, run_cnt, run_tot, pad_row, pad_cnt, n_used, lpos, hn)


def _ffn_body(te_ref, nu_ref, xs_ref, wgu_ref, bgu_ref, wd_ref, bd_ref, ys_ref, wgu_bf, wd_bf):
    i = pl.program_id(0)

    @pl.when((i == 0) | (te_ref[i] != te_ref[jnp.maximum(i, 1) - 1]))
    def _():
        wgu_bf[...] = wgu_ref[...].astype(BF16)
        wd_bf[...] = wd_ref[...].astype(BF16)

    @pl.when(i >= nu_ref[0])
    def _():
        ys_ref[...] = jnp.zeros_like(ys_ref)

    @pl.when(i < nu_ref[0])
    def _():
        gu = jnp.dot(_unpack_bf16_pairs(xs_ref[...]), wgu_bf[...],
                     preferred_element_type=F32) + bgu_ref[...]
        glu = jnp.minimum(gu[:, :D_FF], SWIGLU_LIMIT)
        lin = jnp.clip(gu[:, D_FF:], -SWIGLU_LIMIT, SWIGLU_LIMIT)
        act = glu * _sigmoid(SWIGLU_ALPHA * glu) * (lin + 1.0)
        y = jnp.dot(act.astype(BF16), wd_bf[...], preferred_element_type=F32) + bd_ref[...]
        ys_ref[...] = _pack_bf16_pairs(y.astype(BF16).astype(F32))


def _ffn(tile_expert, n_used, xs, wgu, bgu, wd, bd):
    M_pad = xs.shape[0]
    tm = MOE_TILE
    n_tiles = M_pad // tm
    row = lambda i, te, nu: (jnp.minimum(i, nu[0] - 1), 0)
    return pl.pallas_call(
        _ffn_body,
        grid_spec=pltpu.PrefetchScalarGridSpec(
            num_scalar_prefetch=2,
            grid=(n_tiles,),
            in_specs=[
                pl.BlockSpec((tm, PACKED_W), row),
                pl.BlockSpec((None, D_MODEL, 2 * D_FF), lambda i, te, nu: (te[i], 0, 0)),
                pl.BlockSpec((None, 1, 2 * D_FF), lambda i, te, nu: (te[i], 0, 0)),
                pl.BlockSpec((None, D_FF, D_MODEL), lambda i, te, nu: (te[i], 0, 0)),
                pl.BlockSpec((None, 1, D_MODEL), lambda i, te, nu: (te[i], 0, 0)),
            ],
            out_specs=pl.BlockSpec((tm, PACKED_W), lambda i, te, nu: (i, 0)),
            scratch_shapes=[pltpu.VMEM((D_MODEL, 2 * D_FF), BF16), pltpu.VMEM((D_FF, D_MODEL), BF16)],
        ),
        out_shape=jax.ShapeDtypeStruct((M_pad, PACKED_W), U32),
        compiler_params=_params("arbitrary"),
        name="expert_ffn",
    )(tile_expert, n_used, xs, wgu, bgu, wd, bd)


def _combine_body(row_ref, cnt_ref, tot_ref, lpos_ref, w_ref, h1_ref, ys_ref, o_ref, buf_ref, sems):
    t = pl.program_id(0)
    nt = pl.num_programs(0)
    tm = h1_ref.shape[0]
    slot = t % 2

    def runs(tile, buf_slot, wait):
        def one_expert(e, off):
            n = cnt_ref[tile * N_EXPERTS + e]
            src = row_ref[tile * N_EXPERTS + e]
            def one(sub, size):
                cp = _rows_copy(ys_ref, src + sub, buf_ref.at[buf_slot], off + sub, size,
                                sems.at[buf_slot])
                cp.wait() if wait else cp.start()
            _for_row_chunks(n, one)
            return off + n
        lax.fori_loop(0, N_EXPERTS, one_expert, 0)

    @pl.when(t == 0)
    def _():
        buf_ref[...] = jnp.zeros_like(buf_ref)
        runs(t, slot, False)

    @pl.when(t + 1 < nt)
    def _():
        runs(t + 1, 1 - slot, False)

    _wait_rows(tot_ref[t], TOP_K * tm, ys_ref, buf_ref.at[slot], sems.at[slot])

    yb = _unpack_bf16_pairs(buf_ref[slot])
    col = lax.broadcasted_iota(I32, (tm, _sort_rows(tm)), 1)
    lp = lpos_ref[...]
    w = w_ref[...]
    wmat = jnp.zeros(col.shape, F32)
    for k in range(TOP_K):
        wmat = jnp.where(col == lp[:, k:k + 1], w[:, k:k + 1], wmat)
    w_hi = wmat.astype(BF16)
    w_lo = (wmat - w_hi.astype(F32)).astype(BF16)
    o_ref[...] = (h1_ref[...] + jnp.dot(w_hi, yb, preferred_element_type=F32)
                  + jnp.dot(w_lo, yb, preferred_element_type=F32))


def _combine(run_row, run_cnt, run_tot, lpos_tok, w_tok, h1, ys):
    T = h1.shape[0]
    tm = min(ROUTE_TILE, T)
    return pl.pallas_call(
        _combine_body,
        grid_spec=pltpu.PrefetchScalarGridSpec(
            num_scalar_prefetch=3,
            grid=(T // tm,),
            in_specs=[
                pl.BlockSpec((tm, TOP_K), lambda i, *_: (i, 0)),
                pl.BlockSpec((tm, TOP_K), lambda i, *_: (i, 0)),
                pl.BlockSpec((tm, D_MODEL), lambda i, *_: (i, 0)),
                pl.BlockSpec(memory_space=pl.ANY),
            ],
            out_specs=pl.BlockSpec((tm, D_MODEL), lambda i, *_: (i, 0)),
            scratch_shapes=[
                pltpu.VMEM((2, _sort_rows(tm), PACKED_W), U32),
                pltpu.SemaphoreType.DMA((2,)),
            ],
        ),
        out_shape=jax.ShapeDtypeStruct((T, D_MODEL), F32),
        compiler_params=_params("arbitrary"),
        name="combine",
    )(run_row, run_cnt, run_tot, lpos_tok, w_tok, h1, ys)


def _layer(h, l, p):
    B, S, _ = h.shape
    T = B * S
    lambda_init = 0.8 - 0.6 * math.exp(-0.3 * l)
    hf = h.reshape(T, D_MODEL)

    w_in = p["w_in"]
    sizes = [ATTN_WIDTH, ATTN_WIDTH, ATTN_WIDTH, D_INNER, CONV_DIM, SSM_HEADS, 2 * D_MODEL]
    offs = [0]
    for s in sizes:
        offs.append(offs[-1] + s)
    wq, wk, wv, wz, wxbc, wdt, wg = (w_in[:, offs[n]:offs[n + 1]] for n in range(7))
    w_main = jnp.concatenate([wq, wk, wv, wxbc, wz, wg], axis=1).astype(BF16)
    pad = jnp.zeros((D_MODEL, LANES - 3 * SSM_HEADS), F32)
    w_dt = jnp.concatenate([wdt, wdt, wdt, pad], axis=1).astype(BF16)

    proj, dt_raw = _in_proj(hf, p["norm1_w"].reshape(1, D_MODEL), w_main, w_dt)

    qk_scale = ATTN_HEAD_DIM ** -0.5 * LOG2E
    qg = (jnp.tile(p["q_norm_w"], 2) * qk_scale).reshape(1, LANES)
    kg = jnp.tile(p["k_norm_w"], 2).reshape(1, LANES)
    slopes = jnp.power(2.0, -8.0 * (jnp.arange(ATTN_HEADS, dtype=F32) + 1.0) / ATTN_HEADS) * LOG2E
    c_tab = jnp.broadcast_to(slopes[:, None, None], (ATTN_HEADS, 1, LANES))
    c_row_tab = jnp.broadcast_to(slopes[:, None, None], (ATTN_HEADS, 1, ATTN_TILE))
    lam = (jnp.exp(jnp.sum(p["lambda_q1"] * p["lambda_k1"]))
           - jnp.exp(jnp.sum(p["lambda_q2"] * p["lambda_k2"])) + lambda_init).reshape(1)
    qT, ka, vT = _attn_prep(proj, qg, kg, c_tab, B, S)
    attn = _attention(lam, qT, ka, vT, c_row_tab, p["subln_w"].reshape(LANES, 1), B, S, lambda_init)

    rep3 = lambda v: jnp.concatenate([v, v, v, jnp.zeros((LANES - 3 * SSM_HEADS,), F32)]).reshape(1, LANES)
    head_of_row = jnp.arange(LANES) % SSM_HEADS
    head_of_col = jnp.arange(D_INNER) // SSM_HEAD_DIM
    expand = ((head_of_row[:, None] == head_of_col[None, :])
              & (jnp.arange(LANES)[:, None] < 3 * SSM_HEADS)).astype(BF16)
    ssm = _ssd(proj, dt_raw, p["conv_w"], p["conv_b"].reshape(1, CONV_DIM), rep3(p["dt_bias"]),
               rep3(p["a_log"]), jnp.repeat(p["d_skip"], SSM_HEAD_DIM).reshape(1, D_INNER),
               p["ssm_norm_w"].reshape(1, D_INNER), expand, B, S)

    wr = jnp.pad(p["w_router"], ((0, 0), (0, LANES - N_EXPERTS))).astype(BF16)
    br = jnp.pad(p["b_router"], (0, LANES - N_EXPERTS)).reshape(1, LANES)
    h1, hn, idx, w_top, lpos = _merge(attn, ssm, proj, hf, p["w_attn_proj"].astype(BF16),
                                      p["w_ssm_proj"].astype(BF16), p["w_out"].astype(BF16),
                                      p["norm2_w"].reshape(1, D_MODEL), wr, br)

    tr = min(ROUTE_TILE, T)
    experts = jnp.arange(N_EXPERTS, dtype=I32)
    chosen = idx.reshape(TOP_K, T // tr, tr)[..., None] == experts
    tile_cnt = jnp.sum(chosen, axis=(0, 2), dtype=I32)
    tile_cnt = (tile_cnt + ROW_ALIGN - 1) // ROW_ALIGN * ROW_ALIGN
    counts = jnp.sum(tile_cnt, axis=0)
    tiles_per = (counts + MOE_TILE - 1) // MOE_TILE
    tile_end = jnp.cumsum(tiles_per)
    group_start = (tile_end - tiles_per) * MOE_TILE
    n_runs = (T // tr) * N_EXPERTS
    n_tiles = -(-(T * TOP_K + n_runs * (ROW_ALIGN - 1)) // MOE_TILE) + N_EXPERTS
    M_pad = n_tiles * MOE_TILE
    tile_expert = jnp.minimum(jnp.sum(jnp.arange(n_tiles)[:, None] >= tile_end[None, :], axis=1),
                              N_EXPERTS - 1).astype(I32)
    n_used = tile_end[-1:].astype(I32)
    run_row = (group_start[None, :] + jnp.cumsum(tile_cnt, axis=0) - tile_cnt).reshape(-1)
    run_cnt = tile_cnt.reshape(-1)

    run_tot = jnp.sum(tile_cnt, axis=1)
    xs = _dispatch(run_row, run_cnt, run_tot, group_start + counts,
                   tiles_per * MOE_TILE - counts, n_used, lpos, hn, M_pad)
    ys = _ffn(tile_expert, n_used, xs, p["w_gate_up"],
              p["b_gate_up"].reshape(N_EXPERTS, 1, 2 * D_FF), p["w_down"],
              p["b_down"].reshape(N_EXPERTS, 1, D_MODEL))
    out = _combine(run_row, run_cnt, run_tot, lpos.T, w_top.T, h1, ys)
    return out.reshape(B, S, D_MODEL)


def kernel(x, norm1_w, w_in, q_norm_w, k_norm_w, lambda_q1, lambda_k1, lambda_q2, lambda_k2, subln_w, conv_w, conv_b, dt_bias, a_log, d_skip, ssm_norm_w, w_attn_proj, w_ssm_proj, w_out, norm2_w, w_router, b_router, w_gate_up, b_gate_up, w_down, b_down):
    params = dict(norm1_w=norm1_w, w_in=w_in, q_norm_w=q_norm_w, k_norm_w=k_norm_w,
                  lambda_q1=lambda_q1, lambda_k1=lambda_k1, lambda_q2=lambda_q2, lambda_k2=lambda_k2,
                  subln_w=subln_w, conv_w=conv_w, conv_b=conv_b, dt_bias=dt_bias, a_log=a_log,
                  d_skip=d_skip, ssm_norm_w=ssm_norm_w, w_attn_proj=w_attn_proj,
                  w_ssm_proj=w_ssm_proj, w_out=w_out, norm2_w=norm2_w, w_router=w_router,
                  b_router=b_router, w_gate_up=w_gate_up, b_gate_up=b_gate_up, w_down=w_down,
                  b_down=b_down)
    h = x
    for l in range(w_in.shape[0]):
        h = _layer(h, l, {k: v[l] for k, v in params.items()})
    return h
```
